```python
import math
import jax, jax.numpy as jnp
from jax import lax
import numpy as np

D_MODEL = 1024
BATCH = 2
SEQ = 16384
DEPTH = 2
DEC_BATCH = 8
DEC_SEQ = 64
PAST_LEN = 4096

CHUNK = 64
GROUP_W = D_MODEL // 4
HEAD_DIM = 64
EPS = 1e-6
GLA_HEADS = GROUP_W // HEAD_DIM
GLA_DK = HEAD_DIM // 2
GLA_DV = HEAD_DIM
GLA_RANK = 16
GLA_TAU = 16.0
RG_W = GROUP_W
RG_BLOCKS = 4
RG_BLOCK = RG_W // RG_BLOCKS
RG_CONV = 4
RG_C = 8.0
SB_HEADS = GROUP_W // HEAD_DIM
SB_DH = HEAD_DIM
SB_BLOCK = 128
SC_W = GROUP_W
SC_CONV = 3
D_FF = 4 * D_MODEL

IN_SIZES = (GLA_HEADS * GLA_DK, GLA_HEADS * GLA_DK, GLA_HEADS * GLA_DV, GLA_RANK, GROUP_W,
            RG_W, RG_W,
            SB_HEADS * SB_DH, SB_HEADS * SB_DH, SB_HEADS * SB_DH,
            SC_W, SC_W, SC_W)
N_IN = sum(IN_SIZES)

kernel_name = 'hybrid_stream_gla_rglru_stickbreak_shortconv_step'


def rmsnorm(x, g):
    xf = x.astype(jnp.float32)
    y = xf * lax.rsqrt(jnp.mean(xf * xf, axis=-1, keepdims=True) + EPS)
    return (y * g.astype(jnp.float32)).astype(x.dtype)


def head_rmsnorm(x, g):
    shp = x.shape
    xf = x.astype(jnp.float32).reshape(shp[:-1] + (shp[-1] // HEAD_DIM, HEAD_DIM))
    y = xf * lax.rsqrt(jnp.mean(xf * xf, axis=-1, keepdims=True) + EPS)
    return (y.reshape(shp) * g.astype(jnp.float32)).astype(x.dtype)


def split_cols(z):
    parts, off = [], 0
    for n in IN_SIZES:
        parts.append(z[..., off:off + n])
        off += n
    return parts


def causal_dwconv(x, buf, w):
    width = w.shape[0]
    t = x.shape[1]
    xp = jnp.concatenate([buf.astype(x.dtype), x], axis=1)
    y = xp[:, 0:t] * w[0]
    for i in range(1, width):
        y = y + xp[:, i:i + t] * w[i]
    return y, xp[:, t:]


def _affine_combine(e1, e2):
    a1, b1 = e1
    a2, b2 = e2
    return a1 * a2, a2 * b1 + b2


def linear_scan(a, u, h0):
    a_cum, u_cum = lax.associative_scan(_affine_combine, (a, u), axis=1)
    hs = u_cum + a_cum * h0[:, None]
    return hs, hs[:, -1]


def gla_chunked(s0, q, k, v, log_alpha, chunk):
    b, t, h, dk = q.shape
    dv = v.shape[-1]
    nc = t // chunk
    f32 = jnp.float32
    qc = q.astype(f32).reshape(b, nc, chunk, h, dk)
    kc = k.astype(f32).reshape(b, nc, chunk, h, dk)
    vc = v.astype(f32).reshape(b, nc, chunk, h, dv)
    g = jnp.cumsum(log_alpha.astype(f32).reshape(b, nc, chunk, h, dk), axis=2)
    g_end = g[:, :, -1]
    u = jnp.einsum('bclhk,bclhv->bchkv', kc * jnp.exp(g_end[:, :, None] - g), vc)
    dec = jnp.exp(g_end)[..., None]
    dec_cum, u_cum = lax.associative_scan(_affine_combine, (dec, u), axis=1)
    s = u_cum + dec_cum * s0.astype(f32)[:, None]
    o = jnp.einsum('bclhk,bchkv->bclhv', qc, s).reshape(b, t, h, dv)
    return o, s[:, -1]


def sb_block(qb, qpos, k, v, kpos):
    z = jnp.einsum('bqhd,bkhd->bhqk', qb, k).astype(jnp.float32) * (SB_DH ** -0.5)
    mask = kpos[None, :] < qpos[:, None]
    ls_neg = jax.nn.log_sigmoid(-z)
    ls_pos = ls_neg + z
    lneg = jnp.where(mask, ls_neg, 0.0)
    rest = lax.cumsum(lneg, axis=3, reverse=True) - lneg
    w = jnp.where(mask, jnp.exp(ls_pos + rest), 0.0)
    return jnp.einsum('bhqk,bkhd->bqhd', w, v.astype(jnp.float32))


def stick_breaking(q, k, v, q_pos, k_pos):
    b, t, h, d = q.shape
    if t <= SB_BLOCK:
        return sb_block(q, q_pos, k, v, k_pos)
    nb = t // SB_BLOCK
    qb = q.reshape(b, nb, SB_BLOCK, h, d).transpose(1, 0, 2, 3, 4)
    pb = q_pos.reshape(nb, SB_BLOCK)
    out = lax.map(lambda a: sb_block(a[0], a[1], k, v, k_pos), (qb, pb))
    return out.transpose(1, 0, 2, 3, 4).reshape(b, t, h, d)


def trunk_layer(x, c, state, p):
    (s_gla, h_rg, buf_rg, k_past, v_past, buf_sc) = state
    (ln1_g, ln2_g, w_ada, b_ada, w_in, gla_wa2, gla_ba, rg_conv_w, rg_conv_b,
     rg_wa, rg_ba, rg_wx, rg_bx, rg_lambda, sc_conv_w, out_norm_g, w_out, mlp_w1, mlp_w2) = p
    dt = x.dtype
    b, t, _ = x.shape
    mod = jax.nn.silu(c) @ w_ada + b_ada
    sh1, sc1, g1, sh2, sc2, g2 = jnp.split(mod[:, None, :], 6, axis=-1)
    hn = rmsnorm(x, ln1_g) * (1.0 + sc1) + sh1
    z = hn @ w_in
    (qa, ka, va, lra, ra, gb, xb, qc, kc, vc, bd, cd, xd) = split_cols(z)

    qa = qa.reshape(b, t, GLA_HEADS, GLA_DK) * (GLA_DK ** -0.5)
    ka = ka.reshape(b, t, GLA_HEADS, GLA_DK)
    va = va.reshape(b, t, GLA_HEADS, GLA_DV)
    log_alpha = jax.nn.log_sigmoid((lra @ gla_wa2 + gla_ba).astype(jnp.float32)) / GLA_TAU
    log_alpha = log_alpha.reshape(b, t, GLA_HEADS, GLA_DK)
    chunk = CHUNK if t >= CHUNK else t
    o_a, s_gla_new = gla_chunked(s_gla, qa, ka, va, log_alpha, chunk)
    o_a = (head_rmsnorm(o_a.reshape(b, t, GROUP_W), out_norm_g[0:GROUP_W]) * jax.nn.silu(ra.astype(jnp.float32))).astype(dt)

    xcv, buf_rg_new = causal_dwconv(xb, buf_rg, rg_conv_w)
    xcv = xcv + rg_conv_b
    xblk = xcv.reshape(b, t, RG_BLOCKS, RG_BLOCK)
    r_gate = jax.nn.sigmoid(jnp.einsum('btni,nij->btnj', xblk, rg_wa).reshape(b, t, RG_W) + rg_ba)
    i_gate = jax.nn.sigmoid(jnp.einsum('btni,nij->btnj', xblk, rg_wx).reshape(b, t, RG_W) + rg_bx)
    log_a = -RG_C * r_gate.astype(jnp.float32) * jax.nn.softplus(-rg_lambda.astype(jnp.float32))
    u = jnp.sqrt(-jnp.expm1(2.0 * log_a)) * (i_gate * xcv).astype(jnp.float32)
    hs, h_rg_new = linear_scan(jnp.exp(log_a), u, h_rg.astype(jnp.float32))
    o_b = head_rmsnorm(hs * jax.nn.gelu(gb.astype(jnp.float32)), out_norm_g[GROUP_W:2 * GROUP_W]).astype(dt)

    qc = qc.reshape(b, t, SB_HEADS, SB_DH)
    kc = kc.reshape(b, t, SB_HEADS, SB_DH)
    vc = vc.reshape(b, t, SB_HEADS, SB_DH)
    p_len = k_past.shape[1]
    k_all = jnp.concatenate([k_past.astype(dt), kc], axis=1)
    v_all = jnp.concatenate([v_past.astype(dt), vc], axis=1)
    q_pos = p_len + jnp.arange(t, dtype=jnp.int32)
    k_pos = jnp.arange(p_len + t, dtype=jnp.int32)
    o_c = stick_breaking(qc, k_all, v_all, q_pos, k_pos).reshape(b, t, GROUP_W)
    o_c = head_rmsnorm(o_c, out_norm_g[2 * GROUP_W:3 * GROUP_W]).astype(dt)

    yd, buf_sc_new = causal_dwconv(cd * xd, buf_sc, sc_conv_w)
    o_d = head_rmsnorm(bd * yd, out_norm_g[3 * GROUP_W:]).astype(dt)

    mix = jnp.concatenate([o_a, o_b, o_c, o_d], axis=-1) @ w_out
    x = x + g1 * mix
    h2 = rmsnorm(x, ln2_g) * (1.0 + sc2) + sh2
    x = x + g2 * (jnp.square(jax.nn.relu(h2 @ mlp_w1)) @ mlp_w2)
    new_state = (s_gla_new.astype(dt), h_rg_new.astype(dt), buf_rg_new, kc, vc, buf_sc_new)
    return x, new_state


def setup_inputs(seed: int = 0) -> dict:
    key = jax.random.key(seed)
    k = jax.random.split(key, 30)

    def nrm(kk, shape, scale=1.0):
        return scale * jax.random.normal(kk, shape, jnp.float32)

    a0 = jax.random.uniform(k[23], (DEPTH, RG_W), jnp.float32, minval=0.9, maxval=0.999)
    s_l = a0 ** (1.0 / RG_C)
    return {
        'x_prompt': nrm(k[0], (BATCH, SEQ, D_MODEL)),
        'x_sample': nrm(k[1], (DEC_BATCH, DEC_SEQ, D_MODEL)),
        'c_prompt': nrm(k[2], (BATCH, D_MODEL)),
        'c_sample': nrm(k[3], (DEC_BATCH, D_MODEL)),
        'state_gla': nrm(k[4], (DEPTH, DEC_BATCH, GLA_HEADS, GLA_DK, GLA_DV), 0.5),
        'state_rg_h': nrm(k[5], (DEPTH, DEC_BATCH, RG_W), 0.5),
        'state_rg_conv': nrm(k[6], (DEPTH, DEC_BATCH, RG_CONV - 1, RG_W)),
        'cache_sb_k': nrm(k[7], (DEPTH, DEC_BATCH, PAST_LEN, SB_HEADS, SB_DH)),
        'cache_sb_v': nrm(k[8], (DEPTH, DEC_BATCH, PAST_LEN, SB_HEADS, SB_DH)),
        'state_sc_conv': nrm(k[9], (DEPTH, DEC_BATCH, SC_CONV - 1, SC_W)),
        'ln1_g': 1.0 + nrm(k[10], (DEPTH, D_MODEL), 0.05),
        'ln2_g': 1.0 + nrm(k[11], (DEPTH, D_MODEL), 0.05),
        'w_ada': nrm(k[12], (DEPTH, D_MODEL, 6 * D_MODEL), 0.5 * D_MODEL ** -0.5),
        'b_ada': nrm(k[13], (DEPTH, 6 * D_MODEL), 0.02),
        'w_in': nrm(k[14], (DEPTH, D_MODEL, N_IN), D_MODEL ** -0.5),
        'gla_wa2': nrm(k[15], (DEPTH, GLA_RANK, GLA_HEADS * GLA_DK), GLA_RANK ** -0.5),
        'gla_ba': nrm(k[16], (DEPTH, GLA_HEADS * GLA_DK), 0.1),
        'rg_conv_w': nrm(k[17], (DEPTH, RG_CONV, RG_W), RG_CONV ** -0.5),
        'rg_conv_b': nrm(k[18], (DEPTH, RG_W), 0.02),
        'rg_wa': nrm(k[19], (DEPTH, RG_BLOCKS, RG_BLOCK, RG_BLOCK), RG_BLOCK ** -0.5),
        'rg_ba': nrm(k[20], (DEPTH, RG_W), 0.02),
        'rg_wx': nrm(k[21], (DEPTH, RG_BLOCKS, RG_BLOCK, RG_BLOCK), RG_BLOCK ** -0.5),
        'rg_bx': nrm(k[22], (DEPTH, RG_W), 0.02),
        'rg_lambda': jnp.log(s_l) - jnp.log1p(-s_l),
        'sc_conv_w': nrm(k[24], (DEPTH, SC_CONV, SC_W), SC_CONV ** -0.5),
        'out_norm_g': 1.0 + nrm(k[25], (DEPTH, D_MODEL), 0.05),
        'w_out': nrm(k[26], (DEPTH, D_MODEL, D_MODEL), D_MODEL ** -0.5),
        'mlp_w1': nrm(k[27], (DEPTH, D_MODEL, D_FF), D_MODEL ** -0.5),
        'mlp_w2': nrm(k[28], (DEPTH, D_FF, D_MODEL), D_FF ** -0.5),
        'final_g': 1.0 + nrm(k[29], (D_MODEL,), 0.05),
    }


def reference(x_prompt, x_sample, c_prompt, c_sample, state_gla, state_rg_h, state_rg_conv,
              cache_sb_k, cache_sb_v, state_sc_conv, ln1_g, ln2_g, w_ada, b_ada, w_in,
              gla_wa2, gla_ba, rg_conv_w, rg_conv_b, rg_wa, rg_ba, rg_wx, rg_bx, rg_lambda,
              sc_conv_w, out_norm_g, w_out, mlp_w1, mlp_w2, final_g):
    dt = x_prompt.dtype
    bp = x_prompt.shape[0]
    prompt_state0 = (jnp.zeros((bp, GLA_HEADS, GLA_DK, GLA_DV), dt),
                     jnp.zeros((bp, RG_W), dt),
                     jnp.zeros((bp, RG_CONV - 1, RG_W), dt),
                     jnp.zeros((bp, 0, SB_HEADS, SB_DH), dt),
                     jnp.zeros((bp, 0, SB_HEADS, SB_DH), dt),
                     jnp.zeros((bp, SC_CONV - 1, SC_W), dt))
    xp, xs = x_prompt, x_sample
    p_new, s_new = [], []
    for l in range(DEPTH):
        params = (ln1_g[l], ln2_g[l], w_ada[l], b_ada[l], w_in[l], gla_wa2[l], gla_ba[l],
                  rg_conv_w[l], rg_conv_b[l], rg_wa[l], rg_ba[l], rg_wx[l], rg_bx[l], rg_lambda[l],
                  sc_conv_w[l], out_norm_g[l], w_out[l], mlp_w1[l], mlp_w2[l])
        xp, st_p = trunk_layer(xp, c_prompt, prompt_state0, params)
        sample_state = (state_gla[l], state_rg_h[l], state_rg_conv[l], cache_sb_k[l], cache_sb_v[l], state_sc_conv[l])
        xs, st_s = trunk_layer(xs, c_sample, sample_state, params)
        p_new.append(st_p)
        s_new.append(st_s)

    def stk(lst, i):
        return jnp.stack([st[i] for st in lst], axis=0)

    y_prompt = rmsnorm(xp, final_g)
    y_sample = rmsnorm(xs, final_g)
    return (y_prompt, y_sample,
            stk(p_new, 0), stk(s_new, 0),
            stk(p_new, 1), stk(s_new, 1),
            stk(p_new, 2), stk(s_new, 2),
            stk(p_new, 3), stk(s_new, 3),
            stk(p_new, 4), stk(s_new, 4),
            stk(p_new, 5), stk(s_new, 5))
```

```python
import functools

import jax
import jax.numpy as jnp
from jax import lax
from jax.experimental import pallas as pl
from jax.experimental.pallas import tpu as pltpu

F32 = jnp.float32
BF16 = jnp.bfloat16

D_MODEL = 1024
GROUP_W = 256
HEAD_DIM = 64
N_HEADS = GROUP_W // HEAD_DIM
CHUNK = 64
EPS = 1e-6
GLA_DK = 32
GLA_RANK = 16
GLA_TAU = 16.0
RG_C = 8.0
RG_CONV = 4
SC_CONV = 3
D_FF = 4 * D_MODEL
N_MOD = 6 * D_MODEL

LANE = 128
SUBLANE = 8
VMEM_LIMIT_BYTES = 56 * 1024 * 1024

SEG = {
    "qa": (0, 128), "ka": (128, 256), "va": (256, 512), "lra": (512, 640), "ra": (640, 896),
    "gb": (896, 1152), "xb": (1152, 1408), "qc": (1408, 1664), "kc": (1664, 1920),
    "vc": (1920, 2176), "bd": (2176, 2432), "cd": (2432, 2688), "xd": (2688, 2944),
}
N_IN_PAD = 2944
LRA_END = 2 * N_HEADS * GLA_DK + N_HEADS * HEAD_DIM + GLA_RANK

SB_EXIT = -110.0


def _dot(a, b):
    return jnp.dot(a, b, preferred_element_type=F32)


def _dot_nt(a, b):
    return lax.dot_general(a, b, (((1,), (1,)), ((), ())), preferred_element_type=F32)


def _dot_tn(a, b):
    return lax.dot_general(a, b, (((0,), (0,)), ((), ())), preferred_element_type=F32)


def _rms(x):
    return x * lax.rsqrt(jnp.mean(x * x, axis=-1, keepdims=True) + EPS)


def _sigmoid(x):
    return 1.0 / (1.0 + jnp.exp(-x))


def _softplus(x):
    return jnp.maximum(x, 0.0) + jnp.log(1.0 + jnp.exp(-jnp.abs(x)))


def _gelu_tanh(x):
    c = 0.7978845608028654
    return x * (0.5 * (1.0 + jnp.tanh(c * (x + 0.044715 * (x * x * x)))))


def _split3(x):
    hi = x.astype(BF16)
    r = x - hi.astype(F32)
    mid = r.astype(BF16)
    lo = (r - mid.astype(F32)).astype(BF16)
    return hi, mid, lo


def _shift_rows(x, prev8, s):
    conc = jnp.concatenate([prev8, x], axis=0)
    return pltpu.roll(conc, s, 0)[SUBLANE:, :]


def _mod_kernel(c_ref, w_ref, b_ref, o_ref):
    c = c_ref[...]
    s = (c * _sigmoid(c)).astype(BF16)
    o_ref[0] = _dot(s, w_ref[0].astype(BF16)) + b_ref[0]


def _modulation(c_all, w_ada, b_ada):
    depth = w_ada.shape[0]
    rows = c_all.shape[0]
    tn = 1024
    return pl.pallas_call(
        _mod_kernel,
        grid=(depth, N_MOD // tn),
        in_specs=[
            pl.BlockSpec((rows, D_MODEL), lambda l, j: (0, 0)),
            pl.BlockSpec((1, D_MODEL, tn), lambda l, j: (l, 0, j)),
            pl.BlockSpec((1, 1, tn), lambda l, j: (l, 0, j)),
        ],
        out_specs=pl.BlockSpec((1, rows, tn), lambda l, j: (l, 0, j)),
        out_shape=jax.ShapeDtypeStruct((depth, rows, N_MOD), F32),
        compiler_params=pltpu.CompilerParams(
            dimension_semantics=("arbitrary", "arbitrary"), vmem_limit_bytes=VMEM_LIMIT_BYTES),
        name="adaln_modulation",
    )(c_all, w_ada, b_ada.reshape(depth, 1, N_MOD))


def _mixer_kernel(*refs, tm, tk, tkd, n_tiles, p_len):
    has_past = p_len > 0
    own_scratch = n_tiles > 1
    it = iter(refs)
    x_ref, mod_ref, ln1_ref, win_ref, wa2_ref, gba_ref = (next(it) for _ in range(6))
    rcw_ref, rcb_ref, rwa_ref, rba_ref, rwx_ref, rbx_ref, rlam_ref = (next(it) for _ in range(7))
    scw_ref, ong_ref, wout_ref, hnm_ref, cs_ref, trid_ref, trik_ref = (next(it) for _ in range(7))
    st0_ref, h0_ref, rg0_ref, sc0_ref = (next(it) for _ in range(4))
    if has_past:
        pk_ref, pv_ref = next(it), next(it)
    x1_ref, kc_ref, vc_ref, sto_ref, ho_ref, rgo_ref, sco_ref = (next(it) for _ in range(7))
    hn_scr, st_scr, h_scr, rgp_scr, scp_scr, qm_scr, acc_scr, car_scr = (next(it) for _ in range(8))
    if own_scratch:
        k_scr, v_scr = next(it), next(it)

    i = pl.program_id(1)

    @pl.when(i == 0)
    def _init():
        st_scr[...] = st0_ref[0]
        h_scr[...] = h0_ref[0]
        rgp_scr[...] = rg0_ref[0]
        scp_scr[...] = sc0_ref[0]

    x = x_ref[0]
    mod = mod_ref[0]
    sh1 = mod[:, 0:D_MODEL]
    sc1 = mod[:, D_MODEL:2 * D_MODEL]
    g1 = mod[:, 2 * D_MODEL:3 * D_MODEL]
    hn = (_rms(x) * ln1_ref[...]) * (1.0 + sc1) + sh1
    hn_scr[...] = hn.astype(BF16)

    def proj(name):
        a, b = SEG[name]
        return _dot(hn_scr[...], win_ref[:, a:b])

    ong = ong_ref[...]

    def head_norm(v, group):
        ms = _dot((v * v).astype(BF16), hnm_ref[...])
        return v * lax.rsqrt(ms + EPS) * ong[:, group * GROUP_W:(group + 1) * GROUP_W]

    qa = proj("qa") * (GLA_DK ** -0.5)
    ka = proj("ka")
    va = proj("va")
    lra = proj("lra")
    ra = proj("ra")
    la = -_softplus(-(_dot(lra.astype(BF16), wa2_ref[...]) + gba_ref[...])) * (1.0 / GLA_TAU)
    la_hi, la_mid, la_lo = _split3(la)
    cs = cs_ref[...]
    gg = _dot(cs, la_hi) + _dot(cs, la_mid) + _dot(cs, la_lo)
    g_cum = gg[0:tm]
    g_end = gg[tm:2 * tm]
    kd = (ka * jnp.exp(g_end - g_cum)).astype(BF16)
    dec = jnp.exp(g_end)
    vab = va.astype(BF16)
    qab = qa.astype(BF16)
    st_row = lax.broadcasted_iota(jnp.int32, (GROUP_W, N_HEADS * GLA_DK), 0) >> 6
    st_col = lax.broadcasted_iota(jnp.int32, (GROUP_W, N_HEADS * GLA_DK), 1) >> 5
    st_mask = st_row == st_col
    st = st_scr[...]
    o_chunks = []
    for c in range(tm // CHUNK):
        r0 = c * CHUNK
        ut = _dot_tn(vab[r0:r0 + CHUNK], kd[r0:r0 + CHUNK])
        st = st * dec[r0:r0 + 1] + jnp.where(st_mask, ut, 0.0)
        o_chunks.append(_dot_nt(qab[r0:r0 + CHUNK], st.astype(BF16)))
    st_scr[...] = st
    o_gla = o_chunks[0] if len(o_chunks) == 1 else jnp.concatenate(o_chunks, axis=0)
    o_a = head_norm(o_gla, 0) * (ra * _sigmoid(ra))

    gb = proj("gb")
    xb = proj("xb")
    rgp = rgp_scr[...]
    rcw = rcw_ref[...]
    xcv = _shift_rows(xb, rgp, 3) * rcw[0:1]
    xcv = xcv + _shift_rows(xb, rgp, 2) * rcw[1:2]
    xcv = xcv + _shift_rows(xb, rgp, 1) * rcw[2:3]
    xcv = xcv + xb * rcw[3:4]
    xcv = xcv + rcb_ref[...]
    rgp_scr[...] = xb[tm - SUBLANE:tm]
    xcvb = xcv.astype(BF16)
    r_gate = _sigmoid(_dot(xcvb, rwa_ref[...]) + rba_ref[...])
    i_gate = _sigmoid(_dot(xcvb, rwx_ref[...]) + rbx_ref[...])
    log_a = (-RG_C) * r_gate * _softplus(-rlam_ref[...])
    a_sc = jnp.exp(log_a)
    u_sc = jnp.sqrt(1.0 - jnp.exp(2.0 * log_a)) * (i_gate * xcv)
    row = lax.broadcasted_iota(jnp.int32, (tm, GROUP_W), 0)
    s = 1
    while s < tm:
        keep = row >= s
        a_sh = jnp.where(keep, pltpu.roll(a_sc, s, 0), 1.0)
        u_sh = jnp.where(keep, pltpu.roll(u_sc, s, 0), 0.0)
        u_sc = a_sc * u_sh + u_sc
        a_sc = a_sc * a_sh
        s *= 2
    hs = u_sc + a_sc * h_scr[...]
    h_scr[...] = hs[tm - 1:tm]
    o_b = head_norm(hs * _gelu_tanh(gb), 1)

    bd = proj("bd")
    cx = proj("cd") * proj("xd")
    scp = scp_scr[...]
    scw = scw_ref[...]
    yd = _shift_rows(cx, scp, 2) * scw[0:1]
    yd = yd + _shift_rows(cx, scp, 1) * scw[1:2]
    yd = yd + cx * scw[2:3]
    scp_scr[...] = cx[tm - SUBLANE:tm]
    o_d = head_norm(bd * yd, 3)

    q = proj("qc") * (HEAD_DIM ** -0.5)
    kc = proj("kc")
    vc = proj("vc")
    kc_ref[0] = kc
    vc_ref[0] = vc
    lane_head = lax.broadcasted_iota(jnp.int32, (1, GROUP_W), 1) >> 6
    for h in range(N_HEADS):
        qm_scr[h] = jnp.where(lane_head == h, q, 0.0).astype(BF16)
    kb = kc.astype(BF16)
    vb = vc.astype(BF16)
    if own_scratch:
        row0 = pl.multiple_of(i * tm, tm)
        k_scr[pl.ds(row0, tm), :] = kb
        v_scr[pl.ds(row0, tm), :] = vb
    if tkd > tm:
        pad = jnp.zeros((tkd - tm, GROUP_W), BF16)
        kb = jnp.concatenate([kb, pad], axis=0)
        vb = jnp.concatenate([vb, pad], axis=0)
    acc_scr[...] = jnp.zeros_like(acc_scr)
    car_scr[...] = jnp.zeros_like(car_scr)

    def attn_block(kblk, vblk, tri_ref, masked):
        tkb = kblk.shape[0]
        if masked:
            qrow = lax.broadcasted_iota(jnp.int32, (tm, tkb), 0)
            kcol = lax.broadcasted_iota(jnp.int32, (tm, tkb), 1)
            causal = kcol < qrow
        vrow_head = lax.broadcasted_iota(jnp.int32, (tkb, GROUP_W), 1) >> 6
        ws, vs = [], []
        for h in range(N_HEADS):
            z = _dot_nt(qm_scr[h], kblk)
            lneg = -_softplus(z)
            lpos = lneg + z
            if masked:
                lneg = jnp.where(causal, lneg, 0.0)
            hi = lneg.astype(BF16)
            lo = (lneg - hi.astype(F32)).astype(BF16)
            rt = _dot(jnp.concatenate([hi, lo], axis=1), tri_ref[...])
            rest = rt[:, 0:tkb]
            tot = rt[:, tkb:tkb + LANE]
            car = car_scr[h * tm:(h + 1) * tm, :]
            carb = car if tkb == LANE else jnp.concatenate([car] * (tkb // LANE), axis=1)
            w = jnp.exp(lpos + (rest + carb))
            if masked:
                w = jnp.where(causal, w, 0.0)
            ws.append(w.astype(BF16))
            vs.append(jnp.where(vrow_head == h, vblk, jnp.zeros_like(vblk)))
            car_scr[h * tm:(h + 1) * tm, :] = car + tot
        acc_scr[...] += _dot(jnp.concatenate(ws, axis=1), jnp.concatenate(vs, axis=0))

    attn_block(kb, vb, trid_ref, True)
    n_prev = (p_len // tk) if has_past else i * (tm // tk)
    if has_past or own_scratch:
        def cond(c):
            j, m = c
            return jnp.logical_and(j >= 0, m > SB_EXIT)

        def body(c):
            j, _ = c
            r0 = pl.multiple_of(j * tk, tk)
            if has_past:
                kblk = pk_ref[0, pl.ds(r0, tk), :].astype(BF16)
                vblk = pv_ref[0, pl.ds(r0, tk), :].astype(BF16)
            else:
                kblk = k_scr[pl.ds(r0, tk), :]
                vblk = v_scr[pl.ds(r0, tk), :]
            attn_block(kblk, vblk, trik_ref, False)
            return j - 1, jnp.max(car_scr[...])

        lax.while_loop(cond, body, (n_prev - 1, jnp.max(car_scr[...])))
    o_c = head_norm(acc_scr[...], 2)

    mix = _dot(o_a.astype(BF16), wout_ref[0:GROUP_W, :])
    mix = mix + _dot(o_b.astype(BF16), wout_ref[GROUP_W:2 * GROUP_W, :])
    mix = mix + _dot(o_c.astype(BF16), wout_ref[2 * GROUP_W:3 * GROUP_W, :])
    mix = mix + _dot(o_d.astype(BF16), wout_ref[3 * GROUP_W:4 * GROUP_W, :])
    x1_ref[0] = x + g1 * mix

    @pl.when(i == n_tiles - 1)
    def _fin():
        sto_ref[0] = st_scr[...]
        ho_ref[0] = h_scr[...]
        rgo_ref[0] = rgp_scr[...]
        sco_ref[0] = scp_scr[...]


def _mixer_tiles(t, p_len):
    tm = min(t, 256)
    tk = 256
    tkd = max(tm, LANE)
    assert t % tm == 0 and tm % CHUNK == 0 and tm % SUBLANE == 0
    n_tiles = t // tm
    assert p_len == 0 or n_tiles == 1, "cached keys are only supported with a single row tile"
    assert p_len % tk == 0 and (n_tiles == 1 or tm == tk)
    return tm, tk, tkd, n_tiles


def _tri_ones(n):
    s = jnp.arange(n)[:, None]
    j = jnp.arange(n)[None, :]
    half = jnp.concatenate([(s > j).astype(BF16), jnp.ones((n, n), BF16)], axis=1)
    return jnp.concatenate([half, half], axis=0)


def _cumsum_mats(tm):
    t = jnp.arange(tm)[:, None]
    s = jnp.arange(tm)[None, :]
    same = (t // CHUNK) == (s // CHUNK)
    return jnp.concatenate([(same & (s <= t)).astype(BF16), same.astype(BF16)], axis=0)


def _const_spec(shape):
    nd = len(shape)
    return pl.BlockSpec(shape, lambda b, i: (0,) * nd)


def _mixer(x, mod, w, state, past):
    bsz, t, _ = x.shape
    p_len = 0 if past is None else past[0].shape[1]
    tm, tk, tkd, n_tiles = _mixer_tiles(t, p_len)
    tri_d = _tri_ones(tkd)
    tri_k = _tri_ones(tk)
    cs = _cumsum_mats(tm)
    per_b = lambda b, i: (b, 0, 0)
    tile = lambda b, i: (b, i, 0)

    consts = [w["ln1_g"], w["w_in"], w["wa2"], w["gla_ba"], w["rg_conv_w"], w["rg_conv_b"],
              w["rg_wa"], w["rg_ba"], w["rg_wx"], w["rg_bx"], w["rg_lambda"], w["sc_conv_w"],
              w["out_norm_g"], w["w_out"], w["hnm"], cs, tri_d, tri_k]
    in_specs = [pl.BlockSpec((1, tm, D_MODEL), tile), pl.BlockSpec((1, 1, N_MOD), per_b)]
    in_specs += [_const_spec(c.shape) for c in consts]
    in_specs += [pl.BlockSpec((1,) + s.shape[1:], per_b) for s in state]
    args = [x, mod] + consts + list(state)
    if past is not None:
        in_specs += [pl.BlockSpec((1, p_len, GROUP_W), per_b)] * 2
        args += list(past)

    out_shape = [
        jax.ShapeDtypeStruct((bsz, t, D_MODEL), F32),
        jax.ShapeDtypeStruct((bsz, t, GROUP_W), F32),
        jax.ShapeDtypeStruct((bsz, t, GROUP_W), F32),
    ] + [jax.ShapeDtypeStruct(s.shape, F32) for s in state]
    out_specs = [pl.BlockSpec((1, tm, D_MODEL), tile), pl.BlockSpec((1, tm, GROUP_W), tile),
                 pl.BlockSpec((1, tm, GROUP_W), tile)]
    out_specs += [pl.BlockSpec((1,) + s.shape[1:], per_b) for s in state]

    scratch = [
        pltpu.VMEM((tm, D_MODEL), BF16),
        pltpu.VMEM((GROUP_W, N_HEADS * GLA_DK), F32),
        pltpu.VMEM((1, GROUP_W), F32),
        pltpu.VMEM((SUBLANE, GROUP_W), F32),
        pltpu.VMEM((SUBLANE, GROUP_W), F32),
        pltpu.VMEM((N_HEADS, tm, GROUP_W), BF16),
        pltpu.VMEM((tm, GROUP_W), F32),
        pltpu.VMEM((N_HEADS * tm, LANE), F32),
    ]
    if n_tiles > 1:
        scratch += [pltpu.VMEM((t, GROUP_W), BF16), pltpu.VMEM((t, GROUP_W), BF16)]

    kern = functools.partial(_mixer_kernel, tm=tm, tk=tk, tkd=tkd, n_tiles=n_tiles, p_len=p_len)
    return pl.pallas_call(
        kern,
        grid=(bsz, n_tiles),
        in_specs=in_specs,
        out_specs=out_specs,
        out_shape=out_shape,
        scratch_shapes=scratch,
        compiler_params=pltpu.CompilerParams(
            dimension_semantics=("arbitrary", "arbitrary"), vmem_limit_bytes=VMEM_LIMIT_BYTES),
        name="mixer_t%d_p%d" % (t, p_len),
    )(*args)


def _mlp_kernel(x_ref, mod_ref, g_ref, w1_ref, w2_ref, fg_ref, o_ref, *, final):
    x = x_ref[0]
    mod = mod_ref[0]
    sh2 = mod[:, 3 * D_MODEL:4 * D_MODEL]
    sc2 = mod[:, 4 * D_MODEL:5 * D_MODEL]
    g2 = mod[:, 5 * D_MODEL:6 * D_MODEL]
    h2 = ((_rms(x) * g_ref[...]) * (1.0 + sc2) + sh2).astype(BF16)
    u = jnp.maximum(_dot(h2, w1_ref[...]), 0.0)
    y = x + g2 * _dot((u * u).astype(BF16), w2_ref[...])
    if final:
        y = _rms(y) * fg_ref[...]
    o_ref[0] = y


def _mlp(x, mod, ln2_g, w1, w2, final_g, final):
    bsz, t, _ = x.shape
    tm = min(t, 256)
    assert t % tm == 0
    tile = lambda b, i: (b, i, 0)
    return pl.pallas_call(
        functools.partial(_mlp_kernel, final=final),
        grid=(bsz, t // tm),
        in_specs=[
            pl.BlockSpec((1, tm, D_MODEL), tile),
            pl.BlockSpec((1, 1, N_MOD), lambda b, i: (b, 0, 0)),
            _const_spec(ln2_g.shape), _const_spec(w1.shape), _const_spec(w2.shape),
            _const_spec(final_g.shape),
        ],
        out_specs=pl.BlockSpec((1, tm, D_MODEL), tile),
        out_shape=jax.ShapeDtypeStruct(x.shape, F32),
        compiler_params=pltpu.CompilerParams(
            dimension_semantics=("parallel", "parallel"), vmem_limit_bytes=VMEM_LIMIT_BYTES),
        name="mlp_t%d%s" % (t, "_final" if final else ""),
    )(x, mod, ln2_g, w1, w2, final_g)


def _block_diag(blocks):
    n, a, b = blocks.shape
    eye = jnp.eye(n, dtype=blocks.dtype)
    return jnp.einsum("nij,nm->nimj", blocks, eye).reshape(n * a, n * b)


def _layer_weights(l, ln1_g, w_in, gla_wa2, gla_ba, rg_conv_w, rg_conv_b, rg_wa, rg_ba, rg_wx,
                   rg_bx, rg_lambda, sc_conv_w, out_norm_g, w_out):
    wi = w_in[l]
    w_in_p = jnp.concatenate(
        [wi[:, :LRA_END], jnp.zeros((D_MODEL, LANE - GLA_RANK), wi.dtype), wi[:, LRA_END:]], axis=1)
    wa2 = jnp.zeros((LANE, N_HEADS * GLA_DK), F32).at[:GLA_RANK].set(gla_wa2[l])
    hnm = _block_diag(jnp.full((N_HEADS, HEAD_DIM, HEAD_DIM), 1.0 / HEAD_DIM, F32))
    row = lambda v: v[l].reshape(1, -1)
    return {
        "ln1_g": row(ln1_g), "w_in": w_in_p.astype(BF16), "wa2": wa2.astype(BF16),
        "gla_ba": row(gla_ba), "rg_conv_w": rg_conv_w[l], "rg_conv_b": row(rg_conv_b),
        "rg_wa": _block_diag(rg_wa[l]).astype(BF16), "rg_ba": row(rg_ba),
        "rg_wx": _block_diag(rg_wx[l]).astype(BF16), "rg_bx": row(rg_bx),
        "rg_lambda": row(rg_lambda), "sc_conv_w": sc_conv_w[l], "out_norm_g": row(out_norm_g),
        "w_out": w_out[l].astype(BF16), "hnm": hnm.astype(BF16),
    }


def _state_in(s_gla, h_rg, buf_rg, buf_sc):
    bsz = s_gla.shape[0]
    eye = jnp.eye(N_HEADS, dtype=F32)
    st = jnp.einsum("bhkv,hg->bhvgk", s_gla.astype(F32), eye).reshape(bsz, GROUP_W, N_HEADS * GLA_DK)
    pad8 = lambda buf: jnp.concatenate(
        [jnp.zeros((bsz, SUBLANE - buf.shape[1], GROUP_W), F32), buf.astype(F32)], axis=1)
    return st, h_rg.astype(F32).reshape(bsz, 1, GROUP_W), pad8(buf_rg), pad8(buf_sc)


def _state_out(st, h, rgbuf, scbuf):
    bsz = st.shape[0]
    st5 = st.reshape(bsz, N_HEADS, HEAD_DIM, N_HEADS, GLA_DK)
    s_gla = jnp.stack([st5[:, h_, :, h_, :] for h_ in range(N_HEADS)], axis=1)
    return (jnp.swapaxes(s_gla, 2, 3), h.reshape(bsz, GROUP_W),
            rgbuf[:, SUBLANE - (RG_CONV - 1):], scbuf[:, SUBLANE - (SC_CONV - 1):])


def kernel(x_prompt, x_sample, c_prompt, c_sample, state_gla, state_rg_h, state_rg_conv, cache_sb_k, cache_sb_v, state_sc_conv, ln1_g, ln2_g, w_ada, b_ada, w_in, gla_wa2, gla_ba, rg_conv_w, rg_conv_b, rg_wa, rg_ba, rg_wx, rg_bx, rg_lambda, sc_conv_w, out_norm_g, w_out, mlp_w1, mlp_w2, final_g):
    depth = w_in.shape[0]
    bp, bs = x_prompt.shape[0], x_sample.shape[0]
    p_len = cache_sb_k.shape[2]

    rows = -(-(bp + bs) // 16) * 16
    c_all = jnp.concatenate(
        [c_prompt, c_sample, jnp.zeros((rows - bp - bs, D_MODEL), F32)], axis=0)
    mod_all = _modulation(c_all, w_ada, b_ada)

    zero_state = _state_in(jnp.zeros((bp, N_HEADS, GLA_DK, HEAD_DIM), F32), jnp.zeros((bp, GROUP_W), F32),
                           jnp.zeros((bp, RG_CONV - 1, GROUP_W), F32), jnp.zeros((bp, SC_CONV - 1, GROUP_W), F32))
    fg = final_g.reshape(1, D_MODEL)

    xp, xs = x_prompt, x_sample
    outs_p, outs_s = [], []
    for l in range(depth):
        w = _layer_weights(l, ln1_g, w_in, gla_wa2, gla_ba, rg_conv_w, rg_conv_b, rg_wa, rg_ba,
                           rg_wx, rg_bx, rg_lambda, sc_conv_w, out_norm_g, w_out)
        w1 = mlp_w1[l].astype(BF16)
        w2 = mlp_w2[l].astype(BF16)
        g2 = ln2_g[l].reshape(1, D_MODEL)
        mod_p = mod_all[l, :bp].reshape(bp, 1, N_MOD)
        mod_s = mod_all[l, bp:bp + bs].reshape(bs, 1, N_MOD)
        final = l == depth - 1

        rp = _mixer(xp, mod_p, w, zero_state, None)
        sample_state = _state_in(state_gla[l], state_rg_h[l], state_rg_conv[l], state_sc_conv[l])
        past = (cache_sb_k[l].reshape(bs, p_len, GROUP_W), cache_sb_v[l].reshape(bs, p_len, GROUP_W))
        rs = _mixer(xs, mod_s, w, sample_state, past)

        xp = _mlp(rp[0], mod_p, g2, w1, w2, fg, final)
        xs = _mlp(rs[0], mod_s, g2, w1, w2, fg, final)
        for r, outs in ((rp, outs_p), (rs, outs_s)):
            b_, t_ = r[1].shape[0], r[1].shape[1]
            kc = r[1].reshape(b_, t_, N_HEADS, HEAD_DIM)
            vc = r[2].reshape(b_, t_, N_HEADS, HEAD_DIM)
            outs.append(_state_out(*r[3:7])[:3] + (kc, vc) + _state_out(*r[3:7])[3:])

    def stk(lst, j):
        return jnp.stack([st[j] for st in lst], axis=0)

    res = [xp, xs]
    for j in range(6):
        res += [stk(outs_p, j), stk(outs_s, j)]
    return tuple(res)
```

```python
import functools

import jax
import jax.numpy as jnp
from jax import lax
from jax.experimental import pallas as pl
from jax.experimental.pallas import tpu as pltpu

F32 = jnp.float32
BF16 = jnp.bfloat16

D_MODEL = 1024
GROUP_W = 256
HEAD_DIM = 64
N_HEADS = GROUP_W // HEAD_DIM
CHUNK = 64
EPS = 1e-6
GLA_DK = 32
GLA_RANK = 16
GLA_TAU = 16.0
RG_C = 8.0
RG_CONV = 4
SC_CONV = 3
D_FF = 4 * D_MODEL
N_MOD = 6 * D_MODEL

LANE = 128
SUBLANE = 8
VMEM_LIMIT_BYTES = 56 * 1024 * 1024

SEG = {
    "qa": (0, 128), "ka": (128, 256), "va": (256, 512), "lra": (512, 640), "ra": (640, 896),
    "gb": (896, 1152), "xb": (1152, 1408), "qc": (1408, 1664),
    "bd": (1664, 1920), "cd": (1920, 2176), "xd": (2176, 2432),
}
N_IN_PAD = 2432
LRA_END = 2 * N_HEADS * GLA_DK + N_HEADS * HEAD_DIM + GLA_RANK
SBK_START = LRA_END + 4 * GROUP_W
SBV_START = SBK_START + GROUP_W
SBV_END = SBV_START + GROUP_W

SB_EXIT = -110.0


def _dot(a, b):
    return jnp.dot(a, b, preferred_element_type=F32)


def _dot_nt(a, b):
    return lax.dot_general(a, b, (((1,), (1,)), ((), ())), preferred_element_type=F32)


def _dot_tn(a, b):
    return lax.dot_general(a, b, (((0,), (0,)), ((), ())), preferred_element_type=F32)


def _rms(x):
    return x * lax.rsqrt(jnp.mean(x * x, axis=-1, keepdims=True) + EPS)


def _sigmoid(x):
    return 1.0 / (1.0 + jnp.exp(-x))


def _log1p_exp_neg_abs(x):
    return jnp.log(1.0 + jnp.exp(-jnp.abs(x)))


def _softplus(x):
    return jnp.maximum(x, 0.0) + _log1p_exp_neg_abs(x)


def _gelu_tanh(x):
    c = 0.7978845608028654
    return x * (0.5 * (1.0 + jnp.tanh(c * (x + 0.044715 * (x * x * x)))))


def _split2(x):
    hi = x.astype(BF16)
    lo = (x - hi.astype(F32)).astype(BF16)
    return hi, lo


def _shift_rows(x, prev8, s):
    conc = jnp.concatenate([prev8, x], axis=0)
    return pltpu.roll(conc, s, 0)[SUBLANE:, :]


def _linear_scan(a, u, h0):
    rows = a.shape[0]
    row8 = lax.broadcasted_iota(jnp.int32, a.shape, 0) & (SUBLANE - 1)
    s = 1
    while s < SUBLANE:
        keep = row8 >= s
        a_sh = jnp.where(keep, pltpu.roll(a, s, 0), 1.0)
        u_sh = jnp.where(keep, pltpu.roll(u, s, 0), 0.0)
        u = a * u_sh + u
        a = a * a_sh
        s *= 2
    h = h0
    out = []
    for g in range(rows // SUBLANE):
        sl = slice(g * SUBLANE, (g + 1) * SUBLANE)
        hs = u[sl] + a[sl] * h
        out.append(hs)
        h = hs[SUBLANE - 1:SUBLANE]
    return jnp.concatenate(out, axis=0), h


def _mod_kernel(c_ref, w_ref, b_ref, o_ref):
    c = c_ref[...]
    s = (c * _sigmoid(c)).astype(BF16)
    o_ref[0] = _dot(s, w_ref[0].astype(BF16)) + b_ref[0]


def _modulation(c_all, w_ada, b_ada):
    depth = w_ada.shape[0]
    rows = c_all.shape[0]
    tn = 1024
    return pl.pallas_call(
        _mod_kernel,
        grid=(depth, N_MOD // tn),
        in_specs=[
            pl.BlockSpec((rows, D_MODEL), lambda l, j: (0, 0)),
            pl.BlockSpec((1, D_MODEL, tn), lambda l, j: (l, 0, j)),
            pl.BlockSpec((1, 1, tn), lambda l, j: (l, 0, j)),
        ],
        out_specs=pl.BlockSpec((1, rows, tn), lambda l, j: (l, 0, j)),
        out_shape=jax.ShapeDtypeStruct((depth, rows, N_MOD), F32),
        compiler_params=pltpu.CompilerParams(
            dimension_semantics=("arbitrary", "arbitrary"), vmem_limit_bytes=VMEM_LIMIT_BYTES),
        name="adaln_modulation",
    )(c_all, w_ada, b_ada.reshape(depth, 1, N_MOD))


def _mixer_kernel(*refs, tm, tk, tkd, n_tiles, p_len):
    has_past = p_len > 0
    own_scratch = n_tiles > 1
    it = iter(refs)
    x_ref, mod_ref, ln1_ref, win_ref, wkt_ref, wvt_ref, wa2_ref, gba_ref = (next(it) for _ in range(8))
    rcw_ref, rcb_ref, rwa_ref, rba_ref, rwx_ref, rbx_ref, rlam_ref = (next(it) for _ in range(7))
    scw_ref, ong_ref, wout_ref, hnm_ref, cs_ref, trid_ref, trik_ref = (next(it) for _ in range(7))
    st0_ref, h0_ref, rg0_ref, sc0_ref = (next(it) for _ in range(4))
    if has_past:
        pk_ref, pv_ref = next(it), next(it)
    x1_ref, kct_ref, vct_ref, sto_ref, ho_ref, rgo_ref, sco_ref = (next(it) for _ in range(7))
    hn_scr, st_scr, h_scr, rgp_scr, scp_scr, qm_scr, acc_scr, car_scr = (next(it) for _ in range(8))
    if own_scratch:
        kt_scr, vt_scr = next(it), next(it)

    i = pl.program_id(1)

    @pl.when(i == 0)
    def _init():
        st_scr[...] = st0_ref[0]
        h_scr[...] = h0_ref[0]
        rgp_scr[...] = rg0_ref[0]
        scp_scr[...] = sc0_ref[0]

    x = x_ref[0]
    mod = mod_ref[0]
    sh1 = mod[:, 0:D_MODEL]
    sc1 = mod[:, D_MODEL:2 * D_MODEL]
    g1 = mod[:, 2 * D_MODEL:3 * D_MODEL]
    hn = ((_rms(x) * ln1_ref[...]) * (1.0 + sc1) + sh1).astype(BF16)
    if tkd > tm:
        hn = jnp.concatenate([hn, jnp.zeros((tkd - tm, D_MODEL), BF16)], axis=0)
    hn_scr[...] = hn

    def proj(name):
        a, b = SEG[name]
        return _dot(hn_scr[0:tm, :], win_ref[:, a:b])

    ong = ong_ref[...]

    def head_norm(v, group):
        ms = _dot((v * v).astype(BF16), hnm_ref[...])
        return v * (lax.rsqrt(ms + EPS) * ong[:, group * GROUP_W:(group + 1) * GROUP_W])

    qa = proj("qa") * (GLA_DK ** -0.5)
    ka = proj("ka")
    va = proj("va")
    lra = proj("lra")
    ra = proj("ra")
    la = -_softplus(-(_dot(lra.astype(BF16), wa2_ref[...]) + gba_ref[...])) * (1.0 / GLA_TAU)
    la_hi, la_lo = _split2(la)
    gg = _dot(cs_ref[...], jnp.concatenate([la_hi, la_lo], axis=1))
    gg = gg[:, 0:LANE] + gg[:, LANE:2 * LANE]
    g_cum = gg[0:tm]
    g_end = gg[tm:2 * tm]
    kd = (ka * jnp.exp(g_end - g_cum)).astype(BF16)
    dec = jnp.exp(g_end)
    vab = va.astype(BF16)
    qab = qa.astype(BF16)
    st_row = lax.broadcasted_iota(jnp.int32, (GROUP_W, N_HEADS * GLA_DK), 0) >> 6
    st_col = lax.broadcasted_iota(jnp.int32, (GROUP_W, N_HEADS * GLA_DK), 1) >> 5
    st_mask = st_row == st_col
    n_chunks = tm // CHUNK
    uts = [_dot_tn(vab[c * CHUNK:(c + 1) * CHUNK], kd[c * CHUNK:(c + 1) * CHUNK])
           for c in range(n_chunks)]
    st = st_scr[...]
    sts = []
    for c in range(n_chunks):
        st = st * dec[c * CHUNK:c * CHUNK + 1] + jnp.where(st_mask, uts[c], 0.0)
        sts.append(st.astype(BF16))
    st_scr[...] = st
    o_chunks = [_dot_nt(qab[c * CHUNK:(c + 1) * CHUNK], sts[c]) for c in range(n_chunks)]
    o_gla = o_chunks[0] if n_chunks == 1 else jnp.concatenate(o_chunks, axis=0)
    o_a = head_norm(o_gla, 0) * (ra * _sigmoid(ra))

    gb = proj("gb")
    xb = proj("xb")
    rgp = rgp_scr[...]
    rcw = rcw_ref[...]
    xcv = _shift_rows(xb, rgp, 3) * rcw[0:1]
    xcv = xcv + _shift_rows(xb, rgp, 2) * rcw[1:2]
    xcv = xcv + _shift_rows(xb, rgp, 1) * rcw[2:3]
    xcv = xcv + xb * rcw[3:4]
    xcv = xcv + rcb_ref[...]
    rgp_scr[...] = xb[tm - SUBLANE:tm]
    xcvb = xcv.astype(BF16)
    r_gate = _sigmoid(_dot(xcvb, rwa_ref[...]) + rba_ref[...])
    i_gate = _sigmoid(_dot(xcvb, rwx_ref[...]) + rbx_ref[...])
    log_a = (-RG_C) * r_gate * _softplus(-rlam_ref[...])
    a_sc = jnp.exp(log_a)
    u_sc = jnp.sqrt(1.0 - jnp.exp(2.0 * log_a)) * (i_gate * xcv)
    hs, h_last = _linear_scan(a_sc, u_sc, h_scr[...])
    h_scr[...] = h_last
    o_b = head_norm(hs * _gelu_tanh(gb), 1)

    bd = proj("bd")
    cx = proj("cd") * proj("xd")
    scp = scp_scr[...]
    scw = scw_ref[...]
    yd = _shift_rows(cx, scp, 2) * scw[0:1]
    yd = yd + _shift_rows(cx, scp, 1) * scw[1:2]
    yd = yd + cx * scw[2:3]
    scp_scr[...] = cx[tm - SUBLANE:tm]
    o_d = head_norm(bd * yd, 3)

    q = proj("qc") * (HEAD_DIM ** -0.5)
    kt = _dot_nt(wkt_ref[...], hn_scr[...])
    vt = _dot_nt(wvt_ref[...], hn_scr[...])
    kct_ref[0] = kt[:, 0:tm]
    vct_ref[0] = vt[:, 0:tm]
    lane_head = lax.broadcasted_iota(jnp.int32, (1, GROUP_W), 1) >> 6
    for h in range(N_HEADS):
        qm_scr[h] = jnp.where(lane_head == h, q, 0.0).astype(BF16)
    ktb = kt.astype(BF16)
    vtb = vt.astype(BF16)
    if own_scratch:
        kt_scr[i] = ktb
        vt_scr[i] = vtb
    acc_scr[...] = jnp.zeros_like(acc_scr)
    car_scr[...] = jnp.zeros_like(car_scr)

    def attn_block(ktblk, vtblk, tri_ref, masked):
        tkb = ktblk.shape[1]
        if masked:
            qrow = lax.broadcasted_iota(jnp.int32, (tm, tkb), 0)
            kcol = lax.broadcasted_iota(jnp.int32, (tm, tkb), 1)
            causal = kcol < qrow
        vrow_head = lax.broadcasted_iota(jnp.int32, (GROUP_W, tkb), 0) >> 6
        ws, vs = [], []
        for h in range(N_HEADS):
            z = _dot(qm_scr[h], ktblk)
            l1p = _log1p_exp_neg_abs(z)
            lneg = jnp.minimum(-z, 0.0) - l1p
            lpos = jnp.minimum(z, 0.0) - l1p
            if masked:
                lneg = jnp.where(causal, lneg, 0.0)
            rt = _dot(lneg.astype(BF16), tri_ref[...])
            rest = rt[:, 0:tkb]
            tot = rt[:, tkb:tkb + LANE]
            car = car_scr[h * tm:(h + 1) * tm, :]
            carb = car if tkb == LANE else jnp.concatenate([car] * (tkb // LANE), axis=1)
            w = jnp.exp(lpos + (rest + carb))
            if masked:
                w = jnp.where(causal, w, 0.0)
            ws.append(w.astype(BF16))
            vs.append(jnp.where(vrow_head == h, vtblk, jnp.zeros_like(vtblk)))
            car_scr[h * tm:(h + 1) * tm, :] = car + tot
        acc_scr[...] += _dot_nt(jnp.concatenate(ws, axis=1), jnp.concatenate(vs, axis=1))

    attn_block(ktb, vtb, trid_ref, True)
    n_prev = (p_len // tk) if has_past else i
    if has_past or own_scratch:
        def cond(c):
            j, m = c
            return jnp.logical_and(j >= 0, m > SB_EXIT)

        def body(c):
            j, _ = c
            if has_past:
                c0 = pl.multiple_of(j * tk, tk)
                ktblk = pk_ref[0, 0, :, pl.ds(c0, tk)].astype(BF16)
                vtblk = pv_ref[0, 0, :, pl.ds(c0, tk)].astype(BF16)
            else:
                ktblk = kt_scr[j]
                vtblk = vt_scr[j]
            attn_block(ktblk, vtblk, trik_ref, False)
            return j - 1, jnp.max(car_scr[...])

        lax.while_loop(cond, body, (n_prev - 1, jnp.max(car_scr[...])))
    o_c = head_norm(acc_scr[...], 2)

    mix = _dot(o_a.astype(BF16), wout_ref[0:GROUP_W, :])
    mix = mix + _dot(o_b.astype(BF16), wout_ref[GROUP_W:2 * GROUP_W, :])
    mix = mix + _dot(o_c.astype(BF16), wout_ref[2 * GROUP_W:3 * GROUP_W, :])
    mix = mix + _dot(o_d.astype(BF16), wout_ref[3 * GROUP_W:4 * GROUP_W, :])
    x1_ref[0] = x + g1 * mix

    @pl.when(i == n_tiles - 1)
    def _fin():
        sto_ref[0] = st_scr[...]
        ho_ref[0] = h_scr[...]
        rgo_ref[0] = rgp_scr[...]
        sco_ref[0] = scp_scr[...]


def _mixer_tiles(t, p_len):
    tm = min(t, 256)
    tk = 256
    tkd = max(tm, LANE)
    assert t % tm == 0 and tm % CHUNK == 0 and tm % SUBLANE == 0
    n_tiles = t // tm
    assert p_len == 0 or n_tiles == 1, "cached keys are only supported with a single row tile"
    assert p_len % tk == 0 and (n_tiles == 1 or tm == tk)
    return tm, tk, tkd, n_tiles


def _tri_ones(n):
    s = jnp.arange(n)[:, None]
    j = jnp.arange(n)[None, :]
    return jnp.concatenate([(s > j).astype(BF16), jnp.ones((n, LANE), BF16)], axis=1)


def _cumsum_mats(tm):
    t = jnp.arange(tm)[:, None]
    s = jnp.arange(tm)[None, :]
    same = (t // CHUNK) == (s // CHUNK)
    return jnp.concatenate([(same & (s <= t)).astype(BF16), same.astype(BF16)], axis=0)


def _const_spec(shape):
    nd = len(shape)
    return pl.BlockSpec(shape, lambda b, i: (0,) * nd)


def _mixer(x, mod, w, state, past, layer):
    bsz, t, _ = x.shape
    p_len = 0 if past is None else past[0].shape[3]
    tm, tk, tkd, n_tiles = _mixer_tiles(t, p_len)
    tri_d = _tri_ones(tkd)
    tri_k = _tri_ones(tk)
    cs = _cumsum_mats(tm)
    per_b = lambda b, i: (b, 0, 0)
    tile = lambda b, i: (b, i, 0)
    tile_t = lambda b, i: (b, 0, i)

    consts = [w["ln1_g"], w["w_in"], w["wkt"], w["wvt"], w["wa2"], w["gla_ba"], w["rg_conv_w"],
              w["rg_conv_b"], w["rg_wa"], w["rg_ba"], w["rg_wx"], w["rg_bx"], w["rg_lambda"],
              w["sc_conv_w"], w["out_norm_g"], w["w_out"], w["hnm"], cs, tri_d, tri_k]
    in_specs = [pl.BlockSpec((1, tm, D_MODEL), tile), pl.BlockSpec((1, 1, N_MOD), per_b)]
    in_specs += [_const_spec(c.shape) for c in consts]
    in_specs += [pl.BlockSpec((1,) + s.shape[1:], per_b) for s in state]
    args = [x, mod] + consts + list(state)
    if past is not None:
        in_specs += [pl.BlockSpec((1, 1, GROUP_W, p_len), lambda b, i: (layer, b, 0, 0))] * 2
        args += list(past)

    out_shape = [
        jax.ShapeDtypeStruct((bsz, t, D_MODEL), F32),
        jax.ShapeDtypeStruct((bsz, GROUP_W, t), F32),
        jax.ShapeDtypeStruct((bsz, GROUP_W, t), F32),
    ] + [jax.ShapeDtypeStruct(s.shape, F32) for s in state]
    out_specs = [pl.BlockSpec((1, tm, D_MODEL), tile), pl.BlockSpec((1, GROUP_W, tm), tile_t),
                 pl.BlockSpec((1, GROUP_W, tm), tile_t)]
    out_specs += [pl.BlockSpec((1,) + s.shape[1:], per_b) for s in state]

    scratch = [
        pltpu.VMEM((tkd, D_MODEL), BF16),
        pltpu.VMEM((GROUP_W, N_HEADS * GLA_DK), F32),
        pltpu.VMEM((1, GROUP_W), F32),
        pltpu.VMEM((SUBLANE, GROUP_W), F32),
        pltpu.VMEM((SUBLANE, GROUP_W), F32),
        pltpu.VMEM((N_HEADS, tm, GROUP_W), BF16),
        pltpu.VMEM((tm, GROUP_W), F32),
        pltpu.VMEM((N_HEADS * tm, LANE), F32),
    ]
    if n_tiles > 1:
        scratch += [pltpu.VMEM((n_tiles, GROUP_W, tm), BF16)] * 2

    kern = functools.partial(_mixer_kernel, tm=tm, tk=tk, tkd=tkd, n_tiles=n_tiles, p_len=p_len)
    return pl.pallas_call(
        kern,
        grid=(bsz, n_tiles),
        in_specs=in_specs,
        out_specs=out_specs,
        out_shape=out_shape,
        scratch_shapes=scratch,
        compiler_params=pltpu.CompilerParams(
            dimension_semantics=("arbitrary", "arbitrary"), vmem_limit_bytes=VMEM_LIMIT_BYTES),
        name="mixer_t%d_p%d" % (t, p_len),
    )(*args)


def _mlp_kernel(x_ref, mod_ref, g_ref, w1_ref, w2_ref, fg_ref, o_ref, *, final):
    x = x_ref[0]
    mod = mod_ref[0]
    sh2 = mod[:, 3 * D_MODEL:4 * D_MODEL]
    sc2 = mod[:, 4 * D_MODEL:5 * D_MODEL]
    g2 = mod[:, 5 * D_MODEL:6 * D_MODEL]
    h2 = ((_rms(x) * g_ref[...]) * (1.0 + sc2) + sh2).astype(BF16)
    u = jnp.maximum(_dot(h2, w1_ref[...]), 0.0)
    y = x + g2 * _dot((u * u).astype(BF16), w2_ref[...])
    if final:
        y = _rms(y) * fg_ref[...]
    o_ref[0] = y


def _mlp(x, mod, ln2_g, w1, w2, final_g, final):
    bsz, t, _ = x.shape
    tm = min(t, 256)
    assert t % tm == 0
    tile = lambda b, i: (b, i, 0)
    return pl.pallas_call(
        functools.partial(_mlp_kernel, final=final),
        grid=(bsz, t // tm),
        in_specs=[
            pl.BlockSpec((1, tm, D_MODEL), tile),
            pl.BlockSpec((1, 1, N_MOD), lambda b, i: (b, 0, 0)),
            _const_spec(ln2_g.shape), _const_spec(w1.shape), _const_spec(w2.shape),
            _const_spec(final_g.shape),
        ],
        out_specs=pl.BlockSpec((1, tm, D_MODEL), tile),
        out_shape=jax.ShapeDtypeStruct(x.shape, F32),
        compiler_params=pltpu.CompilerParams(
            dimension_semantics=("parallel", "parallel"), vmem_limit_bytes=VMEM_LIMIT_BYTES),
        name="mlp_t%d%s" % (t, "_final" if final else ""),
    )(x, mod, ln2_g, w1, w2, final_g)


def _block_diag(blocks):
    n, a, b = blocks.shape
    eye = jnp.eye(n, dtype=blocks.dtype)
    return jnp.einsum("nij,nm->nimj", blocks, eye).reshape(n * a, n * b)


def _layer_weights(l, ln1_g, w_in, gla_wa2, gla_ba, rg_conv_w, rg_conv_b, rg_wa, rg_ba, rg_wx,
                   rg_bx, rg_lambda, sc_conv_w, out_norm_g, w_out):
    wi = w_in[l]
    w_in_p = jnp.concatenate(
        [wi[:, :LRA_END], jnp.zeros((D_MODEL, LANE - GLA_RANK), wi.dtype), wi[:, LRA_END:SBK_START],
         wi[:, SBV_END:]], axis=1)
    wa2 = jnp.zeros((LANE, N_HEADS * GLA_DK), F32).at[:GLA_RANK].set(gla_wa2[l])
    hnm = _block_diag(jnp.full((N_HEADS, HEAD_DIM, HEAD_DIM), 1.0 / HEAD_DIM, F32))
    row = lambda v: v[l].reshape(1, -1)
    return {
        "ln1_g": row(ln1_g), "w_in": w_in_p.astype(BF16),
        "wkt": wi[:, SBK_START:SBV_START].T.astype(BF16), "wvt": wi[:, SBV_START:SBV_END].T.astype(BF16),
        "wa2": wa2.astype(BF16),
        "gla_ba": row(gla_ba), "rg_conv_w": rg_conv_w[l], "rg_conv_b": row(rg_conv_b),
        "rg_wa": _block_diag(rg_wa[l]).astype(BF16), "rg_ba": row(rg_ba),
        "rg_wx": _block_diag(rg_wx[l]).astype(BF16), "rg_bx": row(rg_bx),
        "rg_lambda": row(rg_lambda), "sc_conv_w": sc_conv_w[l], "out_norm_g": row(out_norm_g),
        "w_out": w_out[l].astype(BF16), "hnm": hnm.astype(BF16),
    }


def _state_in(s_gla, h_rg, buf_rg, buf_sc):
    bsz = s_gla.shape[0]
    eye = jnp.eye(N_HEADS, dtype=F32)
    st = jnp.einsum("bhkv,hg->bhvgk", s_gla.astype(F32), eye).reshape(bsz, GROUP_W, N_HEADS * GLA_DK)
    pad8 = lambda buf: jnp.concatenate(
        [jnp.zeros((bsz, SUBLANE - buf.shape[1], GROUP_W), F32), buf.astype(F32)], axis=1)
    return st, h_rg.astype(F32).reshape(bsz, 1, GROUP_W), pad8(buf_rg), pad8(buf_sc)


def _state_out(st, h, rgbuf, scbuf):
    bsz = st.shape[0]
    st5 = st.reshape(bsz, N_HEADS, HEAD_DIM, N_HEADS, GLA_DK)
    s_gla = jnp.stack([st5[:, h_, :, h_, :] for h_ in range(N_HEADS)], axis=1)
    return (jnp.swapaxes(s_gla, 2, 3), h.reshape(bsz, GROUP_W),
            rgbuf[:, SUBLANE - (RG_CONV - 1):], scbuf[:, SUBLANE - (SC_CONV - 1):])


def _rows_from_transposed(xt):
    b, _, t = xt.shape
    return jnp.transpose(xt.reshape(b, N_HEADS, HEAD_DIM, t), (0, 3, 1, 2))


def kernel(x_prompt, x_sample, c_prompt, c_sample, state_gla, state_rg_h, state_rg_conv, cache_sb_k, cache_sb_v, state_sc_conv, ln1_g, ln2_g, w_ada, b_ada, w_in, gla_wa2, gla_ba, rg_conv_w, rg_conv_b, rg_wa, rg_ba, rg_wx, rg_bx, rg_lambda, sc_conv_w, out_norm_g, w_out, mlp_w1, mlp_w2, final_g):
    depth = w_in.shape[0]
    bp, bs = x_prompt.shape[0], x_sample.shape[0]
    p_len = cache_sb_k.shape[2]

    rows = -(-(bp + bs) // 16) * 16
    c_all = jnp.concatenate(
        [c_prompt, c_sample, jnp.zeros((rows - bp - bs, D_MODEL), F32)], axis=0)
    mod_all = _modulation(c_all, w_ada, b_ada)

    zero_state = _state_in(jnp.zeros((bp, N_HEADS, GLA_DK, HEAD_DIM), F32), jnp.zeros((bp, GROUP_W), F32),
                           jnp.zeros((bp, RG_CONV - 1, GROUP_W), F32), jnp.zeros((bp, SC_CONV - 1, GROUP_W), F32))
    fg = final_g.reshape(1, D_MODEL)
    past = tuple(jnp.transpose(c, (0, 1, 3, 4, 2)).reshape(depth, bs, GROUP_W, p_len)
                 for c in (cache_sb_k, cache_sb_v))

    xp, xs = x_prompt, x_sample
    outs_p, outs_s = [], []
    for l in range(depth):
        w = _layer_weights(l, ln1_g, w_in, gla_wa2, gla_ba, rg_conv_w, rg_conv_b, rg_wa, rg_ba,
                           rg_wx, rg_bx, rg_lambda, sc_conv_w, out_norm_g, w_out)
        w1 = mlp_w1[l].astype(BF16)
        w2 = mlp_w2[l].astype(BF16)
        g2 = ln2_g[l].reshape(1, D_MODEL)
        mod_p = mod_all[l, :bp].reshape(bp, 1, N_MOD)
        mod_s = mod_all[l, bp:bp + bs].reshape(bs, 1, N_MOD)
        final = l == depth - 1

        rp = _mixer(xp, mod_p, w, zero_state, None, l)
        sample_state = _state_in(state_gla[l], state_rg_h[l], state_rg_conv[l], state_sc_conv[l])
        rs = _mixer(xs, mod_s, w, sample_state, past, l)

        xp = _mlp(rp[0], mod_p, g2, w1, w2, fg, final)
        xs = _mlp(rs[0], mod_s, g2, w1, w2, fg, final)
        for r, outs in ((rp, outs_p), (rs, outs_s)):
            so = _state_out(*r[3:7])
            outs.append(so[:3] + (_rows_from_transposed(r[1]), _rows_from_transposed(r[2])) + so[3:])

    def stk(lst, j):
        return jnp.stack([st[j] for st in lst], axis=0)

    res = [xp, xs]
    for j in range(6):
        res += [stk(outs_p, j), stk(outs_s, j)]
    return tuple(res)
```

```python
import functools

import jax
import jax.numpy as jnp
from jax import lax
from jax.experimental import pallas as pl
from jax.experimental.pallas import tpu as pltpu

F32 = jnp.float32
BF16 = jnp.bfloat16

D_MODEL = 1024
GROUP_W = 256
HEAD_DIM = 64
N_HEADS = GROUP_W // HEAD_DIM
CHUNK = 64
EPS = 1e-6
GLA_DK = 32
GLA_RANK = 16
GLA_TAU = 16.0
RG_C = 8.0
RG_CONV = 4
SC_CONV = 3
D_FF = 4 * D_MODEL
N_MOD = 6 * D_MODEL

LANE = 128
SUBLANE = 8
VMEM_LIMIT_BYTES = 56 * 1024 * 1024

SEG = {
    "qa": (0, 128), "ka": (128, 256), "va": (256, 512), "lra": (512, 640), "ra": (640, 896),
    "gb": (896, 1152), "xb": (1152, 1408), "qc": (1408, 1664),
    "bd": (1664, 1920), "cd": (1920, 2176), "xd": (2176, 2432),
}
N_IN_PAD = 2432
LRA_END = 2 * N_HEADS * GLA_DK + N_HEADS * HEAD_DIM + GLA_RANK
SBK_START = LRA_END + 4 * GROUP_W
SBV_START = SBK_START + GROUP_W
SBV_END = SBV_START + GROUP_W

LOG2E = 1.4426950408889634
SB_EXIT = -110.0 * LOG2E


def _dot(a, b):
    return jnp.dot(a, b, preferred_element_type=F32)


def _dot_nt(a, b):
    return lax.dot_general(a, b, (((1,), (1,)), ((), ())), preferred_element_type=F32)


def _dot_tn(a, b):
    return lax.dot_general(a, b, (((0,), (0,)), ((), ())), preferred_element_type=F32)


def _rms(x):
    return x * lax.rsqrt(jnp.mean(x * x, axis=-1, keepdims=True) + EPS)


def _sigmoid(x):
    return 1.0 / (1.0 + jnp.exp(-x))


def _softplus(x):
    return jnp.maximum(x, 0.0) + jnp.log(1.0 + jnp.exp(-jnp.abs(x)))


def _gelu_tanh(x):
    c = 0.7978845608028654
    return x * (0.5 * (1.0 + jnp.tanh(c * (x + 0.044715 * (x * x * x)))))


def _split2(x):
    hi = x.astype(BF16)
    lo = (x - hi.astype(F32)).astype(BF16)
    return hi, lo


def _shift_rows(x, prev8, s):
    conc = jnp.concatenate([prev8, x], axis=0)
    return pltpu.roll(conc, s, 0)[SUBLANE:, :]


def _linear_scan(a, u, h0):
    rows = a.shape[0]
    row8 = lax.broadcasted_iota(jnp.int32, a.shape, 0) & (SUBLANE - 1)
    s = 1
    while s < SUBLANE:
        keep = row8 >= s
        a_sh = jnp.where(keep, pltpu.roll(a, s, 0), 1.0)
        u_sh = jnp.where(keep, pltpu.roll(u, s, 0), 0.0)
        u = a * u_sh + u
        a = a * a_sh
        s *= 2
    h = h0
    out = []
    for g in range(rows // SUBLANE):
        sl = slice(g * SUBLANE, (g + 1) * SUBLANE)
        hs = u[sl] + a[sl] * h
        out.append(hs)
        h = hs[SUBLANE - 1:SUBLANE]
    return jnp.concatenate(out, axis=0), h


def _mod_kernel(c_ref, w_ref, b_ref, o_ref):
    c = c_ref[...]
    s = (c * _sigmoid(c)).astype(BF16)
    o_ref[0] = _dot(s, w_ref[0].astype(BF16)) + b_ref[0]


def _modulation(c_all, w_ada, b_ada):
    depth = w_ada.shape[0]
    rows = c_all.shape[0]
    tn = 1024
    return pl.pallas_call(
        _mod_kernel,
        grid=(depth, N_MOD // tn),
        in_specs=[
            pl.BlockSpec((rows, D_MODEL), lambda l, j: (0, 0)),
            pl.BlockSpec((1, D_MODEL, tn), lambda l, j: (l, 0, j)),
            pl.BlockSpec((1, 1, tn), lambda l, j: (l, 0, j)),
        ],
        out_specs=pl.BlockSpec((1, rows, tn), lambda l, j: (l, 0, j)),
        out_shape=jax.ShapeDtypeStruct((depth, rows, N_MOD), F32),
        compiler_params=pltpu.CompilerParams(
            dimension_semantics=("arbitrary", "arbitrary"), vmem_limit_bytes=VMEM_LIMIT_BYTES),
        name="adaln_modulation",
    )(c_all, w_ada, b_ada.reshape(depth, 1, N_MOD))


def _mixer_kernel(*refs, tm, tk, tkd, n_tiles, p_len, n_done):
    has_past = p_len > 0
    own_scratch = n_tiles > 1
    it = iter(refs)
    x_ref, mod_ref, ln1_ref, win_ref, wkt_ref, wvt_ref, wa2_ref, gba_ref = (next(it) for _ in range(8))
    rcw_ref, rcb_ref, rwa_ref, rba_ref, rwx_ref, rbx_ref, rlam_ref = (next(it) for _ in range(7))
    scw_ref, ong_ref, wout_ref, hnm_ref, cs_ref, trid_ref, trik_ref = (next(it) for _ in range(7))
    st0_ref, h0_ref, rg0_ref, sc0_ref = (next(it) for _ in range(4))
    if has_past:
        pk_ref, pv_ref = next(it), next(it)
    if n_done:
        dk_ref, dv_ref = next(it), next(it)
    x1_ref, kct_ref, vct_ref, sto_ref, ho_ref, rgo_ref, sco_ref = (next(it) for _ in range(7))
    hn_scr, st_scr, h_scr, rgp_scr, scp_scr, qm_scr, acc_scr, car_scr = (next(it) for _ in range(8))
    if own_scratch:
        kt_scr, vt_scr = next(it), next(it)

    i = pl.program_id(1)

    @pl.when(i == 0)
    def _init():
        st_scr[...] = st0_ref[0]
        h_scr[...] = h0_ref[0]
        rgp_scr[...] = rg0_ref[0]
        scp_scr[...] = sc0_ref[0]

    x = x_ref[0]
    mod = mod_ref[0]
    sh1 = mod[:, 0:D_MODEL]
    sc1 = mod[:, D_MODEL:2 * D_MODEL]
    g1 = mod[:, 2 * D_MODEL:3 * D_MODEL]
    hn = ((_rms(x) * ln1_ref[...]) * (1.0 + sc1) + sh1).astype(BF16)
    if tkd > tm:
        hn = jnp.concatenate([hn, jnp.zeros((tkd - tm, D_MODEL), BF16)], axis=0)
    hn_scr[...] = hn

    def proj(first, last=None):
        a, b = SEG[first][0], SEG[last or first][1]
        return _dot(hn_scr[0:tm, :], win_ref[:, a:b])

    ong = ong_ref[...]

    def head_norm(v, group):
        ms = _dot((v * v).astype(BF16), hnm_ref[...])
        return v * (lax.rsqrt(ms + EPS) * ong[:, group * GROUP_W:(group + 1) * GROUP_W])

    qk = proj("qa", "ka")
    qa = qk[:, 0:LANE] * (GLA_DK ** -0.5)
    ka = qk[:, LANE:2 * LANE]
    va = proj("va")
    lra = proj("lra")
    ra = proj("ra")
    la = -_softplus(-(_dot(lra.astype(BF16), wa2_ref[...]) + gba_ref[...])) * (1.0 / GLA_TAU)
    la_hi, la_lo = _split2(la)
    gg = _dot(cs_ref[...], jnp.concatenate([la_hi, la_lo], axis=1))
    gg = gg[:, 0:LANE] + gg[:, LANE:2 * LANE]
    g_cum = gg[0:tm]
    g_end = gg[tm:2 * tm]
    kd = (ka * jnp.exp(g_end - g_cum)).astype(BF16)
    dec = jnp.exp(g_end)
    vab = va.astype(BF16)
    qab = qa.astype(BF16)
    st_row = lax.broadcasted_iota(jnp.int32, (GROUP_W, N_HEADS * GLA_DK), 0) >> 6
    st_col = lax.broadcasted_iota(jnp.int32, (GROUP_W, N_HEADS * GLA_DK), 1) >> 5
    st_mask = st_row == st_col
    n_chunks = tm // CHUNK
    uts = [_dot_tn(vab[c * CHUNK:(c + 1) * CHUNK], kd[c * CHUNK:(c + 1) * CHUNK])
           for c in range(n_chunks)]
    st = st_scr[...]
    sts = []
    for c in range(n_chunks):
        st = st * dec[c * CHUNK:c * CHUNK + 1] + jnp.where(st_mask, uts[c], 0.0)
        sts.append(st.astype(BF16))
    st_scr[...] = st
    o_chunks = [_dot_nt(qab[c * CHUNK:(c + 1) * CHUNK], sts[c]) for c in range(n_chunks)]
    o_gla = o_chunks[0] if n_chunks == 1 else jnp.concatenate(o_chunks, axis=0)
    o_a = head_norm(o_gla, 0) * (ra * _sigmoid(ra))

    gb = proj("gb")
    xb = proj("xb")
    rgp = rgp_scr[...]
    rcw = rcw_ref[...]
    xcv = _shift_rows(xb, rgp, 3) * rcw[0:1]
    xcv = xcv + _shift_rows(xb, rgp, 2) * rcw[1:2]
    xcv = xcv + _shift_rows(xb, rgp, 1) * rcw[2:3]
    xcv = xcv + xb * rcw[3:4]
    xcv = xcv + rcb_ref[...]
    rgp_scr[...] = xb[tm - SUBLANE:tm]
    xcvb = xcv.astype(BF16)
    r_gate = _sigmoid(_dot(xcvb, rwa_ref[...]) + rba_ref[...])
    i_gate = _sigmoid(_dot(xcvb, rwx_ref[...]) + rbx_ref[...])
    log_a = (-RG_C) * r_gate * _softplus(-rlam_ref[...])
    a_sc = jnp.exp(log_a)
    u_sc = jnp.sqrt(1.0 - jnp.exp(2.0 * log_a)) * (i_gate * xcv)
    hs, h_last = _linear_scan(a_sc, u_sc, h_scr[...])
    h_scr[...] = h_last
    o_b = head_norm(hs * _gelu_tanh(gb), 1)

    bd = proj("bd")
    cx = proj("cd") * proj("xd")
    scp = scp_scr[...]
    scw = scw_ref[...]
    yd = _shift_rows(cx, scp, 2) * scw[0:1]
    yd = yd + _shift_rows(cx, scp, 1) * scw[1:2]
    yd = yd + cx * scw[2:3]
    scp_scr[...] = cx[tm - SUBLANE:tm]
    o_d = head_norm(bd * yd, 3)

    q = proj("qc") * (HEAD_DIM ** -0.5 * LOG2E)
    kt = _dot_nt(wkt_ref[...], hn_scr[...])
    vt = _dot_nt(wvt_ref[...], hn_scr[...])
    for l in range(n_done):
        kct_ref[l, 0] = dk_ref[l, 0]
        vct_ref[l, 0] = dv_ref[l, 0]
    kct_ref[n_done, 0] = kt[:, 0:tm]
    vct_ref[n_done, 0] = vt[:, 0:tm]
    lane_head = lax.broadcasted_iota(jnp.int32, (1, GROUP_W), 1) >> 6
    for h in range(N_HEADS):
        qm_scr[h * tm:(h + 1) * tm, :] = jnp.where(lane_head == h, q, 0.0).astype(BF16)
    ktb = kt.astype(BF16)
    vtb = vt.astype(BF16)
    if own_scratch:
        kt_scr[i] = ktb
        vt_scr[i] = vtb
    acc_scr[...] = jnp.zeros_like(acc_scr)
    car_scr[...] = jnp.zeros_like(car_scr)

    def attn_block(ktblk, vtblk, tri_ref, causal_mask, valid=None):
        tkb = ktblk.shape[1]
        z = _dot(qm_scr[...], ktblk)
        nz = -z
        l1p = jnp.log2(1.0 + jnp.exp2(jnp.minimum(z, nz)))
        lneg = jnp.minimum(nz, 0.0) - l1p
        lpos = lneg + z
        keep = None
        if causal_mask:
            qrow = lax.broadcasted_iota(jnp.int32, (N_HEADS * tm, tkb), 0) & (tm - 1)
            kcol = lax.broadcasted_iota(jnp.int32, (N_HEADS * tm, tkb), 1)
            keep = kcol < qrow
        elif valid is not None:
            keep = valid
        if keep is not None:
            lneg = jnp.where(keep, lneg, 0.0)
        rt = _dot(lneg.astype(BF16), tri_ref[...])
        rest = rt[:, 0:tkb]
        tot = rt[:, tkb:tkb + LANE]
        car = car_scr[...]
        carb = car if tkb == LANE else jnp.concatenate([car] * (tkb // LANE), axis=1)
        w = jnp.exp2(lpos + (rest + carb))
        if keep is not None:
            w = jnp.where(keep, w, 0.0)
        wb = w.astype(BF16)
        car_scr[...] = car + tot
        vrow_head = lax.broadcasted_iota(jnp.int32, (GROUP_W, tkb), 0) >> 6
        ws = [wb[h * tm:(h + 1) * tm] for h in range(N_HEADS)]
        vs = [jnp.where(vrow_head == h, vtblk, jnp.zeros_like(vtblk)) for h in range(N_HEADS)]
        acc_scr[...] += _dot_nt(jnp.concatenate(ws, axis=1), jnp.concatenate(vs, axis=1))

    def older_block(j):
        if has_past:
            c0 = pl.multiple_of(j * tk, tk)
            return (pk_ref[0, 0, :, pl.ds(c0, tk)].astype(BF16),
                    pv_ref[0, 0, :, pl.ds(c0, tk)].astype(BF16))
        return kt_scr[j], vt_scr[j]

    attn_block(ktb, vtb, trid_ref, True)
    if has_past or own_scratch:
        if has_past:
            n_prev = p_len // tk
            kp, vp = older_block(n_prev - 1)
            attn_block(kp, vp, trik_ref, False)
        else:
            n_prev = i
            kp, vp = older_block(jnp.maximum(i - 1, 0))
            attn_block(kp, vp, trik_ref, False, valid=i >= 1)

        def cond(c):
            j, m = c
            return jnp.logical_and(j >= 0, m > SB_EXIT)

        def body(c):
            j, _ = c
            kblk, vblk = older_block(j)
            attn_block(kblk, vblk, trik_ref, False)
            return j - 1, jnp.max(car_scr[...])

        lax.while_loop(cond, body, (n_prev - 2, jnp.max(car_scr[...])))
    o_c = head_norm(acc_scr[...], 2)

    mix = _dot(o_a.astype(BF16), wout_ref[0:GROUP_W, :])
    mix = mix + _dot(o_b.astype(BF16), wout_ref[GROUP_W:2 * GROUP_W, :])
    mix = mix + _dot(o_c.astype(BF16), wout_ref[2 * GROUP_W:3 * GROUP_W, :])
    mix = mix + _dot(o_d.astype(BF16), wout_ref[3 * GROUP_W:4 * GROUP_W, :])
    x1_ref[0] = x + g1 * mix

    @pl.when(i == n_tiles - 1)
    def _fin():
        sto_ref[0] = st_scr[...]
        ho_ref[0] = h_scr[...]
        rgo_ref[0] = rgp_scr[...]
        sco_ref[0] = scp_scr[...]


def _mixer_tiles(t, p_len):
    tm = min(t, 256)
    tk = 256
    tkd = max(tm, LANE)
    assert t % tm == 0 and tm % CHUNK == 0 and tm % SUBLANE == 0 and tm & (tm - 1) == 0
    n_tiles = t // tm
    assert p_len == 0 or n_tiles == 1, "cached keys are only supported with a single row tile"
    assert p_len % tk == 0 and (n_tiles == 1 or tm == tk)
    return tm, tk, tkd, n_tiles


def _tri_ones(n):
    s = jnp.arange(n)[:, None]
    j = jnp.arange(n)[None, :]
    return jnp.concatenate([(s > j).astype(BF16), jnp.ones((n, LANE), BF16)], axis=1)


def _cumsum_mats(tm):
    t = jnp.arange(tm)[:, None]
    s = jnp.arange(tm)[None, :]
    same = (t // CHUNK) == (s // CHUNK)
    return jnp.concatenate([(same & (s <= t)).astype(BF16), same.astype(BF16)], axis=0)


def _const_spec(shape):
    nd = len(shape)
    return pl.BlockSpec(shape, lambda b, i: (0,) * nd)


def _mixer(x, mod, w, state, past, layer, done=None):
    bsz, t, _ = x.shape
    p_len = 0 if past is None else past[0].shape[3]
    n_done = 0 if done is None else done[0].shape[0]
    tm, tk, tkd, n_tiles = _mixer_tiles(t, p_len)
    tri_d = _tri_ones(tkd)
    tri_k = _tri_ones(tk)
    cs = _cumsum_mats(tm)
    per_b = lambda b, i: (b, 0, 0)
    tile = lambda b, i: (b, i, 0)
    tile_t = lambda b, i: (0, b, 0, i)

    consts = [w["ln1_g"], w["w_in"], w["wkt"], w["wvt"], w["wa2"], w["gla_ba"], w["rg_conv_w"],
              w["rg_conv_b"], w["rg_wa"], w["rg_ba"], w["rg_wx"], w["rg_bx"], w["rg_lambda"],
              w["sc_conv_w"], w["out_norm_g"], w["w_out"], w["hnm"], cs, tri_d, tri_k]
    in_specs = [pl.BlockSpec((1, tm, D_MODEL), tile), pl.BlockSpec((1, 1, N_MOD), per_b)]
    in_specs += [_const_spec(c.shape) for c in consts]
    in_specs += [pl.BlockSpec((1,) + s.shape[1:], per_b) for s in state]
    args = [x, mod] + consts + list(state)
    if past is not None:
        in_specs += [pl.BlockSpec((1, 1, GROUP_W, p_len), lambda b, i: (layer, b, 0, 0))] * 2
        args += list(past)
    if n_done:
        in_specs += [pl.BlockSpec((n_done, 1, GROUP_W, tm), tile_t)] * 2
        args += list(done)

    out_shape = [
        jax.ShapeDtypeStruct((bsz, t, D_MODEL), F32),
        jax.ShapeDtypeStruct((n_done + 1, bsz, GROUP_W, t), F32),
        jax.ShapeDtypeStruct((n_done + 1, bsz, GROUP_W, t), F32),
    ] + [jax.ShapeDtypeStruct(s.shape, F32) for s in state]
    out_specs = [pl.BlockSpec((1, tm, D_MODEL), tile),
                 pl.BlockSpec((n_done + 1, 1, GROUP_W, tm), tile_t),
                 pl.BlockSpec((n_done + 1, 1, GROUP_W, tm), tile_t)]
    out_specs += [pl.BlockSpec((1,) + s.shape[1:], per_b) for s in state]

    scratch = [
        pltpu.VMEM((tkd, D_MODEL), BF16),
        pltpu.VMEM((GROUP_W, N_HEADS * GLA_DK), F32),
        pltpu.VMEM((1, GROUP_W), F32),
        pltpu.VMEM((SUBLANE, GROUP_W), F32),
        pltpu.VMEM((SUBLANE, GROUP_W), F32),
        pltpu.VMEM((N_HEADS * tm, GROUP_W), BF16),
        pltpu.VMEM((tm, GROUP_W), F32),
        pltpu.VMEM((N_HEADS * tm, LANE), F32),
    ]
    if n_tiles > 1:
        scratch += [pltpu.VMEM((n_tiles, GROUP_W, tm), BF16)] * 2

    kern = functools.partial(_mixer_kernel, tm=tm, tk=tk, tkd=tkd, n_tiles=n_tiles, p_len=p_len,
                             n_done=n_done)
    return pl.pallas_call(
        kern,
        grid=(bsz, n_tiles),
        in_specs=in_specs,
        out_specs=out_specs,
        out_shape=out_shape,
        scratch_shapes=scratch,
        compiler_params=pltpu.CompilerParams(
            dimension_semantics=("arbitrary", "arbitrary"), vmem_limit_bytes=VMEM_LIMIT_BYTES),
        name="mixer_t%d_p%d_l%d" % (t, p_len, n_done),
    )(*args)


def _mlp_kernel(x_ref, mod_ref, g_ref, w1_ref, w2_ref, fg_ref, o_ref, *, final):
    x = x_ref[0]
    mod = mod_ref[0]
    sh2 = mod[:, 3 * D_MODEL:4 * D_MODEL]
    sc2 = mod[:, 4 * D_MODEL:5 * D_MODEL]
    g2 = mod[:, 5 * D_MODEL:6 * D_MODEL]
    h2 = ((_rms(x) * g_ref[...]) * (1.0 + sc2) + sh2).astype(BF16)
    u = jnp.maximum(_dot(h2, w1_ref[...]), 0.0)
    y = x + g2 * _dot((u * u).astype(BF16), w2_ref[...])
    if final:
        y = _rms(y) * fg_ref[...]
    o_ref[0] = y


def _mlp(x, mod, ln2_g, w1, w2, final_g, final):
    bsz, t, _ = x.shape
    tm = min(t, 256)
    assert t % tm == 0
    tile = lambda b, i: (b, i, 0)
    return pl.pallas_call(
        functools.partial(_mlp_kernel, final=final),
        grid=(bsz, t // tm),
        in_specs=[
            pl.BlockSpec((1, tm, D_MODEL), tile),
            pl.BlockSpec((1, 1, N_MOD), lambda b, i: (b, 0, 0)),
            _const_spec(ln2_g.shape), _const_spec(w1.shape), _const_spec(w2.shape),
            _const_spec(final_g.shape),
        ],
        out_specs=pl.BlockSpec((1, tm, D_MODEL), tile),
        out_shape=jax.ShapeDtypeStruct(x.shape, F32),
        compiler_params=pltpu.CompilerParams(
            dimension_semantics=("parallel", "parallel"), vmem_limit_bytes=VMEM_LIMIT_BYTES),
        name="mlp_t%d%s" % (t, "_final" if final else ""),
    )(x, mod, ln2_g, w1, w2, final_g)


def _block_diag(blocks):
    n, a, b = blocks.shape
    eye = jnp.eye(n, dtype=blocks.dtype)
    return jnp.einsum("nij,nm->nimj", blocks, eye).reshape(n * a, n * b)


def _layer_weights(l, ln1_g, w_in, gla_wa2, gla_ba, rg_conv_w, rg_conv_b, rg_wa, rg_ba, rg_wx,
                   rg_bx, rg_lambda, sc_conv_w, out_norm_g, w_out):
    wi = w_in[l]
    w_in_p = jnp.concatenate(
        [wi[:, :LRA_END], jnp.zeros((D_MODEL, LANE - GLA_RANK), wi.dtype), wi[:, LRA_END:SBK_START],
         wi[:, SBV_END:]], axis=1)
    wa2 = jnp.zeros((LANE, N_HEADS * GLA_DK), F32).at[:GLA_RANK].set(gla_wa2[l])
    hnm = _block_diag(jnp.full((N_HEADS, HEAD_DIM, HEAD_DIM), 1.0 / HEAD_DIM, F32))
    row = lambda v: v[l].reshape(1, -1)
    return {
        "ln1_g": row(ln1_g), "w_in": w_in_p.astype(BF16),
        "wkt": wi[:, SBK_START:SBV_START].T.astype(BF16), "wvt": wi[:, SBV_START:SBV_END].T.astype(BF16),
        "wa2": wa2.astype(BF16),
        "gla_ba": row(gla_ba), "rg_conv_w": rg_conv_w[l], "rg_conv_b": row(rg_conv_b),
        "rg_wa": _block_diag(rg_wa[l]).astype(BF16), "rg_ba": row(rg_ba),
        "rg_wx": _block_diag(rg_wx[l]).astype(BF16), "rg_bx": row(rg_bx),
        "rg_lambda": row(rg_lambda), "sc_conv_w": sc_conv_w[l], "out_norm_g": row(out_norm_g),
        "w_out": w_out[l].astype(BF16), "hnm": hnm.astype(BF16),
    }


def _state_in(s_gla, h_rg, buf_rg, buf_sc):
    bsz = s_gla.shape[0]
    eye = jnp.eye(N_HEADS, dtype=F32)
    st = jnp.einsum("bhkv,hg->bhvgk", s_gla.astype(F32), eye).reshape(bsz, GROUP_W, N_HEADS * GLA_DK)
    pad8 = lambda buf: jnp.concatenate(
        [jnp.zeros((bsz, SUBLANE - buf.shape[1], GROUP_W), F32), buf.astype(F32)], axis=1)
    return st, h_rg.astype(F32).reshape(bsz, 1, GROUP_W), pad8(buf_rg), pad8(buf_sc)


def _state_out(st, h, rgbuf, scbuf):
    bsz = st.shape[0]
    st5 = st.reshape(bsz, N_HEADS, HEAD_DIM, N_HEADS, GLA_DK)
    s_gla = jnp.stack([st5[:, h_, :, h_, :] for h_ in range(N_HEADS)], axis=1)
    return (jnp.swapaxes(s_gla, 2, 3), h.reshape(bsz, GROUP_W),
            rgbuf[:, SUBLANE - (RG_CONV - 1):], scbuf[:, SUBLANE - (SC_CONV - 1):])


def _rows_from_transposed(xt):
    l, b, _, t = xt.shape
    return jnp.transpose(xt.reshape(l, b, N_HEADS, HEAD_DIM, t), (0, 1, 4, 2, 3))


def kernel(x_prompt, x_sample, c_prompt, c_sample, state_gla, state_rg_h, state_rg_conv, cache_sb_k, cache_sb_v, state_sc_conv, ln1_g, ln2_g, w_ada, b_ada, w_in, gla_wa2, gla_ba, rg_conv_w, rg_conv_b, rg_wa, rg_ba, rg_wx, rg_bx, rg_lambda, sc_conv_w, out_norm_g, w_out, mlp_w1, mlp_w2, final_g):
    depth = w_in.shape[0]
    bp, bs = x_prompt.shape[0], x_sample.shape[0]
    p_len = cache_sb_k.shape[2]

    rows = -(-(bp + bs) // 16) * 16
    c_all = jnp.concatenate(
        [c_prompt, c_sample, jnp.zeros((rows - bp - bs, D_MODEL), F32)], axis=0)
    mod_all = _modulation(c_all, w_ada, b_ada)

    zero_state = _state_in(jnp.zeros((bp, N_HEADS, GLA_DK, HEAD_DIM), F32), jnp.zeros((bp, GROUP_W), F32),
                           jnp.zeros((bp, RG_CONV - 1, GROUP_W), F32), jnp.zeros((bp, SC_CONV - 1, GROUP_W), F32))
    fg = final_g.reshape(1, D_MODEL)
    past = tuple(jnp.transpose(c, (0, 1, 3, 4, 2)).reshape(depth, bs, GROUP_W, p_len)
                 for c in (cache_sb_k, cache_sb_v))

    xp, xs = x_prompt, x_sample
    kv_p, kv_s = None, None
    states_p, states_s = [], []
    for l in range(depth):
        w = _layer_weights(l, ln1_g, w_in, gla_wa2, gla_ba, rg_conv_w, rg_conv_b, rg_wa, rg_ba,
                           rg_wx, rg_bx, rg_lambda, sc_conv_w, out_norm_g, w_out)
        w1 = mlp_w1[l].astype(BF16)
        w2 = mlp_w2[l].astype(BF16)
        g2 = ln2_g[l].reshape(1, D_MODEL)
        mod_p = mod_all[l, :bp].reshape(bp, 1, N_MOD)
        mod_s = mod_all[l, bp:bp + bs].reshape(bs, 1, N_MOD)
        final = l == depth - 1

        rp = _mixer(xp, mod_p, w, zero_state, None, l, kv_p)
        sample_state = _state_in(state_gla[l], state_rg_h[l], state_rg_conv[l], state_sc_conv[l])
        rs = _mixer(xs, mod_s, w, sample_state, past, l, kv_s)
        kv_p, kv_s = (rp[1], rp[2]), (rs[1], rs[2])

        xp = _mlp(rp[0], mod_p, g2, w1, w2, fg, final)
        xs = _mlp(rs[0], mod_s, g2, w1, w2, fg, final)
        states_p.append(_state_out(*rp[3:7]))
        states_s.append(_state_out(*rs[3:7]))

    def stk(lst, j):
        return jnp.stack([st[j] for st in lst], axis=0)

    res = [xp, xs]
    for j in range(3):
        res += [stk(states_p, j), stk(states_s, j)]
    res += [_rows_from_transposed(kv_p[0]), _rows_from_transposed(kv_s[0]),
            _rows_from_transposed(kv_p[1]), _rows_from_transposed(kv_s[1]),
            stk(states_p, 3), stk(states_s, 3)]
    return tuple(res)
```

```python
import functools

import jax
import jax.numpy as jnp
from jax import lax
from jax.experimental import pallas as pl
from jax.experimental.pallas import tpu as pltpu

F32 = jnp.float32
BF16 = jnp.bfloat16

D_MODEL = 1024
GROUP_W = 256
HEAD_DIM = 64
N_HEADS = GROUP_W // HEAD_DIM
CHUNK = 64
EPS = 1e-6
GLA_DK = 32
GLA_RANK = 16
GLA_TAU = 16.0
RG_C = 8.0
RG_CONV = 4
SC_CONV = 3
D_FF = 4 * D_MODEL
N_MOD = 6 * D_MODEL

LANE = 128
SUBLANE = 8
VMEM_LIMIT_BYTES = 56 * 1024 * 1024
TILE_ROWS = 256
MLP_SHORT_ROWS = 512

SEG = {
    "qa": (0, 128), "ka": (128, 256), "va": (256, 512), "lra": (512, 640), "ra": (640, 896),
    "gb": (896, 1152), "xb": (1152, 1408), "qc": (1408, 1664),
    "bd": (1664, 1920), "cd": (1920, 2176), "xd": (2176, 2432),
}
N_IN_PAD = 2432
LRA_END = 2 * N_HEADS * GLA_DK + N_HEADS * HEAD_DIM + GLA_RANK
SBK_START = LRA_END + 4 * GROUP_W
SBV_START = SBK_START + GROUP_W
SBV_END = SBV_START + GROUP_W

LOG2E = 1.4426950408889634
SB_EXIT = -110.0 * LOG2E
SB_MASKED = -1.0e9


def _dot(a, b):
    return jnp.dot(a, b, preferred_element_type=F32)


def _dot_nt(a, b):
    return lax.dot_general(a, b, (((1,), (1,)), ((), ())), preferred_element_type=F32)


def _dot_tn(a, b):
    return lax.dot_general(a, b, (((0,), (0,)), ((), ())), preferred_element_type=F32)


def _rms(x):
    return x * lax.rsqrt(jnp.mean(x * x, axis=-1, keepdims=True) + EPS)


def _sigmoid(x):
    return 1.0 / (1.0 + jnp.exp(-x))


def _softplus(x):
    return jnp.maximum(x, 0.0) + jnp.log(1.0 + jnp.exp(-jnp.abs(x)))


def _gelu_tanh(x):
    c = 0.7978845608028654
    return x * (0.5 * (1.0 + jnp.tanh(c * (x + 0.044715 * (x * x * x)))))


def _split2(x):
    hi = x.astype(BF16)
    lo = (x - hi.astype(F32)).astype(BF16)
    return hi, lo


def _shift_rows(x, prev8, s):
    conc = jnp.concatenate([prev8, x], axis=0)
    return pltpu.roll(conc, s, 0)[SUBLANE:, :]


def _linear_scan(a, u, h0):
    rows, cols = a.shape
    a = a.reshape(rows // SUBLANE, SUBLANE, cols)
    u = u.reshape(rows // SUBLANE, SUBLANE, cols)
    row8 = lax.broadcasted_iota(jnp.int32, a.shape, 1)
    s = 1
    while s < SUBLANE:
        keep = row8 >= s
        a_sh = jnp.where(keep, pltpu.roll(a, s, 1), 1.0)
        u_sh = jnp.where(keep, pltpu.roll(u, s, 1), 0.0)
        u = a * u_sh + u
        a = a * a_sh
        s *= 2
    a = a.reshape(rows, cols)
    u = u.reshape(rows, cols)
    h = h0
    out = []
    for g in range(rows // SUBLANE):
        sl = slice(g * SUBLANE, (g + 1) * SUBLANE)
        hs = u[sl] + a[sl] * h
        out.append(hs)
        h = hs[SUBLANE - 1:SUBLANE]
    return jnp.concatenate(out, axis=0), h


def _mod_kernel(c_ref, w_ref, b_ref, o_ref):
    c = c_ref[...]
    s = (c * _sigmoid(c)).astype(BF16)
    o_ref[0] = _dot(s, w_ref[0].astype(BF16)) + b_ref[0]


def _modulation(c_all, w_ada, b_ada):
    depth = w_ada.shape[0]
    rows = c_all.shape[0]
    tn = 1024
    return pl.pallas_call(
        _mod_kernel,
        grid=(depth, N_MOD // tn),
        in_specs=[
            pl.BlockSpec((rows, D_MODEL), lambda l, j: (0, 0)),
            pl.BlockSpec((1, D_MODEL, tn), lambda l, j: (l, 0, j)),
            pl.BlockSpec((1, 1, tn), lambda l, j: (l, 0, j)),
        ],
        out_specs=pl.BlockSpec((1, rows, tn), lambda l, j: (l, 0, j)),
        out_shape=jax.ShapeDtypeStruct((depth, rows, N_MOD), F32),
        compiler_params=pltpu.CompilerParams(
            dimension_semantics=("arbitrary", "arbitrary"), vmem_limit_bytes=VMEM_LIMIT_BYTES),
        name="adaln_modulation",
    )(c_all, w_ada, b_ada.reshape(depth, 1, N_MOD))


def _mixer_kernel(*refs, tm, tk, tkd, n_tiles, p_len, n_done):
    has_past = p_len > 0
    own_scratch = n_tiles > 1
    it = iter(refs)
    x_ref, mod_ref, ln1_ref, win_ref, wkt_ref, wvt_ref, wa2_ref, gba_ref = (next(it) for _ in range(8))
    rcw_ref, rcb_ref, rwa_ref, rba_ref, rwx_ref, rbx_ref, rlam_ref = (next(it) for _ in range(7))
    scw_ref, ong_ref, wout_ref, hnm_ref, cs_ref, trid_ref, trik_ref = (next(it) for _ in range(7))
    st0_ref, h0_ref, rg0_ref, sc0_ref = (next(it) for _ in range(4))
    if has_past:
        pk_ref, pv_ref = next(it), next(it)
    if n_done:
        dk_ref, dv_ref = next(it), next(it)
    x1_ref, kct_ref, vct_ref, sto_ref, ho_ref, rgo_ref, sco_ref = (next(it) for _ in range(7))
    hn_scr, st_scr, h_scr, rgp_scr, scp_scr, qm_scr, acc_scr, car_scr = (next(it) for _ in range(8))
    if own_scratch:
        kt_scr, vt_scr = next(it), next(it)

    i = pl.program_id(1)

    @pl.when(i == 0)
    def _init():
        st_scr[...] = st0_ref[0]
        h_scr[...] = h0_ref[0]
        rgp_scr[...] = rg0_ref[0]
        scp_scr[...] = sc0_ref[0]

    x = x_ref[0]
    mod = mod_ref[0]
    sh1 = mod[:, 0:D_MODEL]
    sc1 = mod[:, D_MODEL:2 * D_MODEL]
    g1 = mod[:, 2 * D_MODEL:3 * D_MODEL]
    hn = ((_rms(x) * ln1_ref[...]) * (1.0 + sc1) + sh1).astype(BF16)
    if tkd > tm:
        hn = jnp.concatenate([hn, jnp.zeros((tkd - tm, D_MODEL), BF16)], axis=0)
    hn_scr[...] = hn

    def proj(first, last=None):
        a, b = SEG[first][0], SEG[last or first][1]
        return _dot(hn_scr[0:tm, :], win_ref[:, a:b])

    ong = ong_ref[...]

    def head_norm(v, group):
        ms = _dot((v * v).astype(BF16), hnm_ref[...])
        return v * (lax.rsqrt(ms + EPS) * ong[:, group * GROUP_W:(group + 1) * GROUP_W])


    xb = proj("xb")
    rgp = rgp_scr[...]
    rcw = rcw_ref[...]
    xcv = _shift_rows(xb, rgp, 3) * rcw[0:1]
    xcv = xcv + _shift_rows(xb, rgp, 2) * rcw[1:2]
    xcv = xcv + _shift_rows(xb, rgp, 1) * rcw[2:3]
    xcv = xcv + xb * rcw[3:4]
    xcv = xcv + rcb_ref[...]
    rgp_scr[...] = xb[tm - SUBLANE:tm]
    q = proj("qc") * (HEAD_DIM ** -0.5 * LOG2E)
    gb = proj("gb")
    xcvb = xcv.astype(BF16)
    r_gate = _sigmoid(_dot(xcvb, rwa_ref[...]) + rba_ref[...])
    i_gate = _sigmoid(_dot(xcvb, rwx_ref[...]) + rbx_ref[...])
    kt = _dot_nt(wkt_ref[...], hn_scr[...])
    vt = _dot_nt(wvt_ref[...], hn_scr[...])
    log_a = (-RG_C) * r_gate * _softplus(-rlam_ref[...])
    a_sc = jnp.exp(log_a)
    u_sc = jnp.sqrt(1.0 - jnp.exp(2.0 * log_a)) * (i_gate * xcv)

    for l in range(n_done):
        kct_ref[l, 0] = dk_ref[l, 0]
        vct_ref[l, 0] = dv_ref[l, 0]
    kct_ref[n_done, 0] = kt[:, 0:tm]
    vct_ref[n_done, 0] = vt[:, 0:tm]
    lane_head = lax.broadcasted_iota(jnp.int32, (1, GROUP_W), 1) >> 6
    for h in range(N_HEADS):
        qm_scr[h * tm:(h + 1) * tm, :] = jnp.where(lane_head == h, q, 0.0).astype(BF16)
    ktb = kt.astype(BF16)
    vtb = vt.astype(BF16)
    if own_scratch:
        kt_scr[i] = ktb
        vt_scr[i] = vtb
    acc_scr[...] = jnp.zeros_like(acc_scr)
    car_scr[...] = jnp.zeros_like(car_scr)

    hs, h_last = _linear_scan(a_sc, u_sc, h_scr[...])
    h_scr[...] = h_last
    o_b = head_norm(hs * _gelu_tanh(gb), 1)

    pending = {
        "qk": lambda: proj("qa", "ka"), "va": lambda: proj("va"), "lra": lambda: proj("lra"),
        "ra": lambda: proj("ra"), "bd": lambda: proj("bd"), "cd": lambda: proj("cd"),
        "xd": lambda: proj("xd"),
    }
    issued = {}

    def issue_next():
        if pending:
            name = next(iter(pending))
            issued[name] = pending.pop(name)()

    def projected(name):
        if name not in issued:
            issued[name] = pending.pop(name)()
        return issued[name]

    def attn_block(ktblk, vtblk, tri_ref, causal_mask, valid=None, between=lambda: None):
        tkb = ktblk.shape[1]
        z_all = _dot(qm_scr[...], ktblk)
        lnegs, lposs = [], []
        for h in range(N_HEADS):
            z = z_all[h * tm:(h + 1) * tm]
            if causal_mask:
                qrow = lax.broadcasted_iota(jnp.int32, (tm, tkb), 0)
                kcol = lax.broadcasted_iota(jnp.int32, (tm, tkb), 1)
                z = jnp.where(kcol < qrow, z, SB_MASKED)
            nz = -z
            l1p = jnp.log2(1.0 + jnp.exp2(jnp.minimum(z, nz)))
            lneg = jnp.minimum(nz, 0.0) - l1p
            lposs.append(lneg + z)
            lnegs.append(lneg.astype(BF16))
            between()
        rt = _dot(jnp.concatenate(lnegs, axis=0), tri_ref[...])
        car = car_scr[...]
        car_scr[...] = car + rt[:, tkb:tkb + LANE]
        vrow_head = lax.broadcasted_iota(jnp.int32, (GROUP_W, tkb), 0) >> 6
        ws, vs = [], []
        for h in range(N_HEADS):
            rows = slice(h * tm, (h + 1) * tm)
            carh = car[rows]
            carb = carh if tkb == LANE else jnp.concatenate([carh] * (tkb // LANE), axis=1)
            ws.append(jnp.exp2(lposs[h] + (rt[rows, 0:tkb] + carb)).astype(BF16))
            v_h = jnp.where(vrow_head == h, vtblk, jnp.zeros_like(vtblk))
            if valid is not None:
                v_h = jnp.where(valid, v_h, jnp.zeros_like(v_h))
            vs.append(v_h)
            between()
        acc_scr[...] += _dot_nt(jnp.concatenate(ws, axis=1), jnp.concatenate(vs, axis=1))

    def older_block(j):
        if has_past:
            c0 = pl.multiple_of(j * tk, tk)
            return (pk_ref[0, 0, :, pl.ds(c0, tk)].astype(BF16),
                    pv_ref[0, 0, :, pl.ds(c0, tk)].astype(BF16))
        return kt_scr[j], vt_scr[j]

    attn_block(ktb, vtb, trid_ref, True, between=issue_next)
    if has_past or own_scratch:
        if has_past:
            n_prev = p_len // tk
            kp, vp = older_block(n_prev - 1)
            attn_block(kp, vp, trik_ref, False, between=issue_next)
        else:
            n_prev = i
            kp, vp = older_block(jnp.maximum(i - 1, 0))
            attn_block(kp, vp, trik_ref, False, valid=i >= 1, between=issue_next)

    qk = projected("qk")
    va = projected("va")
    lra = projected("lra")
    ra = projected("ra")
    qa = qk[:, 0:LANE] * (GLA_DK ** -0.5)
    ka = qk[:, LANE:2 * LANE]
    la = -_softplus(-(_dot(lra.astype(BF16), wa2_ref[...]) + gba_ref[...])) * (1.0 / GLA_TAU)
    la_hi, la_lo = _split2(la)
    gg = _dot(cs_ref[...], jnp.concatenate([la_hi, la_lo], axis=1))
    gg = gg[:, 0:LANE] + gg[:, LANE:2 * LANE]
    g_cum = gg[0:tm]
    g_end = gg[tm:2 * tm]
    kd = (ka * jnp.exp(g_end - g_cum)).astype(BF16)
    dec = jnp.exp(g_end)
    vab = va.astype(BF16)
    qab = qa.astype(BF16)
    st_row = lax.broadcasted_iota(jnp.int32, (GROUP_W, N_HEADS * GLA_DK), 0) >> 6
    st_col = lax.broadcasted_iota(jnp.int32, (GROUP_W, N_HEADS * GLA_DK), 1) >> 5
    st_mask = st_row == st_col
    n_chunks = tm // CHUNK
    uts = [_dot_tn(vab[c * CHUNK:(c + 1) * CHUNK], kd[c * CHUNK:(c + 1) * CHUNK])
           for c in range(n_chunks)]
    st = st_scr[...]
    sts = []
    for c in range(n_chunks):
        st = st * dec[c * CHUNK:c * CHUNK + 1] + jnp.where(st_mask, uts[c], 0.0)
        sts.append(st.astype(BF16))
    st_scr[...] = st
    o_chunks = [_dot_nt(qab[c * CHUNK:(c + 1) * CHUNK], sts[c]) for c in range(n_chunks)]
    o_gla = o_chunks[0] if n_chunks == 1 else jnp.concatenate(o_chunks, axis=0)
    o_a = head_norm(o_gla, 0) * (ra * _sigmoid(ra))

    bd = projected("bd")
    cx = projected("cd") * projected("xd")
    scp = scp_scr[...]
    scw = scw_ref[...]
    yd = _shift_rows(cx, scp, 2) * scw[0:1]
    yd = yd + _shift_rows(cx, scp, 1) * scw[1:2]
    yd = yd + cx * scw[2:3]
    scp_scr[...] = cx[tm - SUBLANE:tm]
    o_d = head_norm(bd * yd, 3)

    if has_past or own_scratch:
        def cond(c):
            j, m = c
            return jnp.logical_and(j >= 0, m > SB_EXIT)

        def body(c):
            j, _ = c
            kblk, vblk = older_block(j)
            attn_block(kblk, vblk, trik_ref, False)
            return j - 1, jnp.max(car_scr[...])

        lax.while_loop(cond, body, (n_prev - 2, jnp.max(car_scr[...])))
    o_c = head_norm(acc_scr[...], 2)

    mix = _dot(o_a.astype(BF16), wout_ref[0:GROUP_W, :])
    mix = mix + _dot(o_b.astype(BF16), wout_ref[GROUP_W:2 * GROUP_W, :])
    mix = mix + _dot(o_c.astype(BF16), wout_ref[2 * GROUP_W:3 * GROUP_W, :])
    mix = mix + _dot(o_d.astype(BF16), wout_ref[3 * GROUP_W:4 * GROUP_W, :])
    x1_ref[0] = x + g1 * mix

    @pl.when(i == n_tiles - 1)
    def _fin():
        sto_ref[0] = st_scr[...]
        ho_ref[0] = h_scr[...]
        rgo_ref[0] = rgp_scr[...]
        sco_ref[0] = scp_scr[...]


def _mixer_tiles(t, p_len):
    tm = min(t, TILE_ROWS)
    tk = TILE_ROWS
    tkd = max(tm, LANE)
    assert t % tm == 0 and tm % CHUNK == 0 and tm % SUBLANE == 0
    n_tiles = t // tm
    assert p_len == 0 or n_tiles == 1, "cached keys are only supported with a single row tile"
    assert p_len % tk == 0 and (n_tiles == 1 or tm == tk)
    return tm, tk, tkd, n_tiles


def _tri_ones(n):
    s = jnp.arange(n)[:, None]
    j = jnp.arange(n)[None, :]
    return jnp.concatenate([(s > j).astype(BF16), jnp.ones((n, LANE), BF16)], axis=1)


def _cumsum_mats(tm):
    t = jnp.arange(tm)[:, None]
    s = jnp.arange(tm)[None, :]
    same = (t // CHUNK) == (s // CHUNK)
    return jnp.concatenate([(same & (s <= t)).astype(BF16), same.astype(BF16)], axis=0)


def _const_spec(shape):
    nd = len(shape)
    return pl.BlockSpec(shape, lambda b, i: (0,) * nd)


def _mixer(x, mod, w, state, past, layer, done=None):
    bsz, t, _ = x.shape
    p_len = 0 if past is None else past[0].shape[3]
    n_done = 0 if done is None else done[0].shape[0]
    tm, tk, tkd, n_tiles = _mixer_tiles(t, p_len)
    tri_d = _tri_ones(tkd)
    tri_k = _tri_ones(tk)
    cs = _cumsum_mats(tm)
    per_b = lambda b, i: (b, 0, 0)
    tile = lambda b, i: (b, i, 0)
    tile_t = lambda b, i: (0, b, 0, i)

    consts = [w["ln1_g"], w["w_in"], w["wkt"], w["wvt"], w["wa2"], w["gla_ba"], w["rg_conv_w"],
              w["rg_conv_b"], w["rg_wa"], w["rg_ba"], w["rg_wx"], w["rg_bx"], w["rg_lambda"],
              w["sc_conv_w"], w["out_norm_g"], w["w_out"], w["hnm"], cs, tri_d, tri_k]
    in_specs = [pl.BlockSpec((1, tm, D_MODEL), tile), pl.BlockSpec((1, 1, N_MOD), per_b)]
    in_specs += [_const_spec(c.shape) for c in consts]
    in_specs += [pl.BlockSpec((1,) + s.shape[1:], per_b) for s in state]
    args = [x, mod] + consts + list(state)
    if past is not None:
        in_specs += [pl.BlockSpec((1, 1, GROUP_W, p_len), lambda b, i: (layer, b, 0, 0))] * 2
        args += list(past)
    if n_done:
        in_specs += [pl.BlockSpec((n_done, 1, GROUP_W, tm), tile_t)] * 2
        args += list(done)

    out_shape = [
        jax.ShapeDtypeStruct((bsz, t, D_MODEL), F32),
        jax.ShapeDtypeStruct((n_done + 1, bsz, GROUP_W, t), F32),
        jax.ShapeDtypeStruct((n_done + 1, bsz, GROUP_W, t), F32),
    ] + [jax.ShapeDtypeStruct(s.shape, F32) for s in state]
    out_specs = [pl.BlockSpec((1, tm, D_MODEL), tile),
                 pl.BlockSpec((n_done + 1, 1, GROUP_W, tm), tile_t),
                 pl.BlockSpec((n_done + 1, 1, GROUP_W, tm), tile_t)]
    out_specs += [pl.BlockSpec((1,) + s.shape[1:], per_b) for s in state]

    scratch = [
        pltpu.VMEM((tkd, D_MODEL), BF16),
        pltpu.VMEM((GROUP_W, N_HEADS * GLA_DK), F32),
        pltpu.VMEM((1, GROUP_W), F32),
        pltpu.VMEM((SUBLANE, GROUP_W), F32),
        pltpu.VMEM((SUBLANE, GROUP_W), F32),
        pltpu.VMEM((N_HEADS * tm, GROUP_W), BF16),
        pltpu.VMEM((tm, GROUP_W), F32),
        pltpu.VMEM((N_HEADS * tm, LANE), F32),
    ]
    if n_tiles > 1:
        scratch += [pltpu.VMEM((n_tiles, GROUP_W, tm), BF16)] * 2

    kern = functools.partial(_mixer_kernel, tm=tm, tk=tk, tkd=tkd, n_tiles=n_tiles, p_len=p_len,
                             n_done=n_done)
    return pl.pallas_call(
        kern,
        grid=(bsz, n_tiles),
        in_specs=in_specs,
        out_specs=out_specs,
        out_shape=out_shape,
        scratch_shapes=scratch,
        compiler_params=pltpu.CompilerParams(
            dimension_semantics=("arbitrary", "arbitrary"), vmem_limit_bytes=VMEM_LIMIT_BYTES),
        name="mixer_t%d_p%d_l%d" % (t, p_len, n_done),
    )(*args)


def _mlp_kernel(x_ref, mod_ref, g_ref, w1_ref, w2_ref, fg_ref, o_ref, *, final):
    x = x_ref[...]
    bb, tm, _ = x.shape
    mod = mod_ref[...]
    sh2 = mod[:, :, 3 * D_MODEL:4 * D_MODEL]
    sc2 = mod[:, :, 4 * D_MODEL:5 * D_MODEL]
    g2 = mod[:, :, 5 * D_MODEL:6 * D_MODEL]
    h2 = ((_rms(x) * g_ref[...]) * (1.0 + sc2) + sh2).astype(BF16).reshape(bb * tm, D_MODEL)
    u = jnp.maximum(_dot(h2, w1_ref[...]), 0.0)
    y = x + g2 * _dot((u * u).astype(BF16), w2_ref[...]).reshape(bb, tm, D_MODEL)
    if final:
        y = _rms(y) * fg_ref[...]
    o_ref[...] = y


def _mlp(x, mod, ln2_g, w1, w2, final_g, final):
    bsz, t, _ = x.shape
    tm = min(t, TILE_ROWS)
    bb = 1 if tm == TILE_ROWS else min(bsz, MLP_SHORT_ROWS // tm)
    assert t % tm == 0 and bsz % bb == 0
    tile = lambda b, i: (b, i, 0)
    return pl.pallas_call(
        functools.partial(_mlp_kernel, final=final),
        grid=(bsz // bb, t // tm),
        in_specs=[
            pl.BlockSpec((bb, tm, D_MODEL), tile),
            pl.BlockSpec((bb, 1, N_MOD), lambda b, i: (b, 0, 0)),
            _const_spec(ln2_g.shape), _const_spec(w1.shape), _const_spec(w2.shape),
            _const_spec(final_g.shape),
        ],
        out_specs=pl.BlockSpec((bb, tm, D_MODEL), tile),
        out_shape=jax.ShapeDtypeStruct(x.shape, F32),
        compiler_params=pltpu.CompilerParams(
            dimension_semantics=("parallel", "parallel"), vmem_limit_bytes=VMEM_LIMIT_BYTES),
        name="mlp_t%d%s" % (t, "_final" if final else ""),
    )(x, mod, ln2_g, w1, w2, final_g)


def _block_diag(blocks):
    n, a, b = blocks.shape
    eye = jnp.eye(n, dtype=blocks.dtype)
    return jnp.einsum("nij,nm->nimj", blocks, eye).reshape(n * a, n * b)


def _layer_weights(l, ln1_g, w_in, gla_wa2, gla_ba, rg_conv_w, rg_conv_b, rg_wa, rg_ba, rg_wx,
                   rg_bx, rg_lambda, sc_conv_w, out_norm_g, w_out):
    wi = w_in[l]
    w_in_p = jnp.concatenate(
        [wi[:, :LRA_END], jnp.zeros((D_MODEL, LANE - GLA_RANK), wi.dtype), wi[:, LRA_END:SBK_START],
         wi[:, SBV_END:]], axis=1)
    wa2 = jnp.zeros((LANE, N_HEADS * GLA_DK), F32).at[:GLA_RANK].set(gla_wa2[l])
    hnm = _block_diag(jnp.full((N_HEADS, HEAD_DIM, HEAD_DIM), 1.0 / HEAD_DIM, F32))
    row = lambda v: v[l].reshape(1, -1)
    return {
        "ln1_g": row(ln1_g), "w_in": w_in_p.astype(BF16),
        "wkt": wi[:, SBK_START:SBV_START].T.astype(BF16), "wvt": wi[:, SBV_START:SBV_END].T.astype(BF16),
        "wa2": wa2.astype(BF16),
        "gla_ba": row(gla_ba), "rg_conv_w": rg_conv_w[l], "rg_conv_b": row(rg_conv_b),
        "rg_wa": _block_diag(rg_wa[l]).astype(BF16), "rg_ba": row(rg_ba),
        "rg_wx": _block_diag(rg_wx[l]).astype(BF16), "rg_bx": row(rg_bx),
        "rg_lambda": row(rg_lambda), "sc_conv_w": sc_conv_w[l], "out_norm_g": row(out_norm_g),
        "w_out": w_out[l].astype(BF16), "hnm": hnm.astype(BF16),
    }


def _state_in(s_gla, h_rg, buf_rg, buf_sc):
    bsz = s_gla.shape[0]
    eye = jnp.eye(N_HEADS, dtype=F32)
    st = jnp.einsum("bhkv,hg->bhvgk", s_gla.astype(F32), eye).reshape(bsz, GROUP_W, N_HEADS * GLA_DK)
    pad8 = lambda buf: jnp.concatenate(
        [jnp.zeros((bsz, SUBLANE - buf.shape[1], GROUP_W), F32), buf.astype(F32)], axis=1)
    return st, h_rg.astype(F32).reshape(bsz, 1, GROUP_W), pad8(buf_rg), pad8(buf_sc)


def _state_out(st, h, rgbuf, scbuf):
    bsz = st.shape[0]
    st5 = st.reshape(bsz, N_HEADS, HEAD_DIM, N_HEADS, GLA_DK)
    s_gla = jnp.stack([st5[:, h_, :, h_, :] for h_ in range(N_HEADS)], axis=1)
    return (jnp.swapaxes(s_gla, 2, 3), h.reshape(bsz, GROUP_W),
            rgbuf[:, SUBLANE - (RG_CONV - 1):], scbuf[:, SUBLANE - (SC_CONV - 1):])


def _rows_from_transposed(xt):
    l, b, _, t = xt.shape
    return jnp.transpose(xt.reshape(l, b, N_HEADS, HEAD_DIM, t), (0, 1, 4, 2, 3))


def kernel(x_prompt, x_sample, c_prompt, c_sample, state_gla, state_rg_h, state_rg_conv, cache_sb_k, cache_sb_v, state_sc_conv, ln1_g, ln2_g, w_ada, b_ada, w_in, gla_wa2, gla_ba, rg_conv_w, rg_conv_b, rg_wa, rg_ba, rg_wx, rg_bx, rg_lambda, sc_conv_w, out_norm_g, w_out, mlp_w1, mlp_w2, final_g):
    depth = w_in.shape[0]
    bp, bs = x_prompt.shape[0], x_sample.shape[0]
    p_len = cache_sb_k.shape[2]

    rows = -(-(bp + bs) // 16) * 16
    c_all = jnp.concatenate(
        [c_prompt, c_sample, jnp.zeros((rows - bp - bs, D_MODEL), F32)], axis=0)
    mod_all = _modulation(c_all, w_ada, b_ada)

    zero_state = _state_in(jnp.zeros((bp, N_HEADS, GLA_DK, HEAD_DIM), F32), jnp.zeros((bp, GROUP_W), F32),
                           jnp.zeros((bp, RG_CONV - 1, GROUP_W), F32), jnp.zeros((bp, SC_CONV - 1, GROUP_W), F32))
    fg = final_g.reshape(1, D_MODEL)
    past = tuple(jnp.transpose(c, (0, 1, 3, 4, 2)).reshape(depth, bs, GROUP_W, p_len)
                 for c in (cache_sb_k, cache_sb_v))

    xp, xs = x_prompt, x_sample
    kv_p, kv_s = None, None
    states_p, states_s = [], []
    for l in range(depth):
        w = _layer_weights(l, ln1_g, w_in, gla_wa2, gla_ba, rg_conv_w, rg_conv_b, rg_wa, rg_ba,
                           rg_wx, rg_bx, rg_lambda, sc_conv_w, out_norm_g, w_out)
        w1 = mlp_w1[l].astype(BF16)
        w2 = mlp_w2[l].astype(BF16)
        g2 = ln2_g[l].reshape(1, D_MODEL)
        mod_p = mod_all[l, :bp].reshape(bp, 1, N_MOD)
        mod_s = mod_all[l, bp:bp + bs].reshape(bs, 1, N_MOD)
        final = l == depth - 1

        rp = _mixer(xp, mod_p, w, zero_state, None, l, kv_p)
        sample_state = _state_in(state_gla[l], state_rg_h[l], state_rg_conv[l], state_sc_conv[l])
        rs = _mixer(xs, mod_s, w, sample_state, past, l, kv_s)
        kv_p, kv_s = (rp[1], rp[2]), (rs[1], rs[2])

        xp = _mlp(rp[0], mod_p, g2, w1, w2, fg, final)
        xs = _mlp(rs[0], mod_s, g2, w1, w2, fg, final)
        states_p.append(_state_out(*rp[3:7]))
        states_s.append(_state_out(*rs[3:7]))

    def stk(lst, j):
        return jnp.stack([st[j] for st in lst], axis=0)

    res = [xp, xs]
    for j in range(3):
        res += [stk(states_p, j), stk(states_s, j)]
    res += [_rows_from_transposed(kv_p[0]), _rows_from_transposed(kv_s[0]),
            _rows_from_transposed(kv_p[1]), _rows_from_transposed(kv_s[1]),
            stk(states_p, 3), stk(states_s, 3)]
    return tuple(res)
```

```python
import functools

import jax
import jax.numpy as jnp
from jax import lax
from jax.experimental import pallas as pl
from jax.experimental.pallas import tpu as pltpu

F32 = jnp.float32
BF16 = jnp.bfloat16

D_MODEL = 1024
GROUP_W = 256
HEAD_DIM = 64
N_HEADS = GROUP_W // HEAD_DIM
CHUNK = 64
EPS = 1e-6
GLA_DK = 32
GLA_RANK = 16
GLA_TAU = 16.0
RG_C = 8.0
RG_CONV = 4
SC_CONV = 3
D_FF = 4 * D_MODEL
N_MOD = 6 * D_MODEL

LANE = 128
SUBLANE = 8
VMEM_LIMIT_BYTES = 56 * 1024 * 1024
MIXER_ROWS = 512
MLP_ROWS = 512
KEY_BLOCK = 256

SEG = {
    "qa": (0, 128), "ka": (128, 256), "va": (256, 512), "lra": (512, 640), "ra": (640, 896),
    "gb": (896, 1152), "xb": (1152, 1408), "qc": (1408, 1664),
    "bd": (1664, 1920), "cd": (1920, 2176), "xd": (2176, 2432),
}
N_IN_PAD = 2432
LRA_END = 2 * N_HEADS * GLA_DK + N_HEADS * HEAD_DIM + GLA_RANK
SBK_START = LRA_END + 4 * GROUP_W
SBV_START = SBK_START + GROUP_W
SBV_END = SBV_START + GROUP_W

LOG2E = 1.4426950408889634
SB_EXIT = -110.0 * LOG2E
SB_MASKED = -1.0e9


def _dot(a, b):
    return jnp.dot(a, b, preferred_element_type=F32)


def _dot_nt(a, b):
    return lax.dot_general(a, b, (((1,), (1,)), ((), ())), preferred_element_type=F32)


def _dot_tn(a, b):
    return lax.dot_general(a, b, (((0,), (0,)), ((), ())), preferred_element_type=F32)


def _rms(x):
    return x * lax.rsqrt(jnp.mean(x * x, axis=-1, keepdims=True) + EPS)


def _sigmoid(x):
    return 1.0 / (1.0 + jnp.exp(-x))


def _softplus(x):
    return jnp.maximum(x, 0.0) + jnp.log(1.0 + jnp.exp(-jnp.abs(x)))


def _gelu_tanh(x):
    c = 0.7978845608028654
    return x * (0.5 * (1.0 + jnp.tanh(c * (x + 0.044715 * (x * x * x)))))


def _split2(x):
    hi = x.astype(BF16)
    lo = (x - hi.astype(F32)).astype(BF16)
    return hi, lo


def _shift_rows(x, prev8, s):
    conc = jnp.concatenate([prev8, x], axis=0)
    return pltpu.roll(conc, s, 0)[SUBLANE:, :]


def _linear_scan(a, u, h0):
    rows, cols = a.shape
    a = a.reshape(rows // SUBLANE, SUBLANE, cols)
    u = u.reshape(rows // SUBLANE, SUBLANE, cols)
    row8 = lax.broadcasted_iota(jnp.int32, a.shape, 1)
    s = 1
    while s < SUBLANE:
        keep = row8 >= s
        a_sh = jnp.where(keep, pltpu.roll(a, s, 1), 1.0)
        u_sh = jnp.where(keep, pltpu.roll(u, s, 1), 0.0)
        u = a * u_sh + u
        a = a * a_sh
        s *= 2
    a = a.reshape(rows, cols)
    u = u.reshape(rows, cols)
    h = h0
    out = []
    for g in range(rows // SUBLANE):
        sl = slice(g * SUBLANE, (g + 1) * SUBLANE)
        hs = u[sl] + a[sl] * h
        out.append(hs)
        h = hs[SUBLANE - 1:SUBLANE]
    return jnp.concatenate(out, axis=0), h


def _mod_kernel(c_ref, w_ref, b_ref, o_ref):
    c = c_ref[...]
    s = (c * _sigmoid(c)).astype(BF16)
    o_ref[0] = _dot(s, w_ref[0].astype(BF16)) + b_ref[0]


def _modulation(c_all, w_ada, b_ada):
    depth = w_ada.shape[0]
    rows = c_all.shape[0]
    tn = 1024
    return pl.pallas_call(
        _mod_kernel,
        grid=(depth, N_MOD // tn),
        in_specs=[
            pl.BlockSpec((rows, D_MODEL), lambda l, j: (0, 0)),
            pl.BlockSpec((1, D_MODEL, tn), lambda l, j: (l, 0, j)),
            pl.BlockSpec((1, 1, tn), lambda l, j: (l, 0, j)),
        ],
        out_specs=pl.BlockSpec((1, rows, tn), lambda l, j: (l, 0, j)),
        out_shape=jax.ShapeDtypeStruct((depth, rows, N_MOD), F32),
        compiler_params=pltpu.CompilerParams(
            dimension_semantics=("arbitrary", "arbitrary"), vmem_limit_bytes=VMEM_LIMIT_BYTES),
        name="adaln_modulation",
    )(c_all, w_ada, b_ada.reshape(depth, 1, N_MOD))


def _mixer_kernel(*refs, tm, ta, tk, tkd, n_tiles, p_len, n_done):
    has_past = p_len > 0
    own_scratch = n_tiles > 1
    n_sub = tm // ta
    it = iter(refs)
    x_ref, mod_ref, ln1_ref, win_ref, wkt_ref, wvt_ref, wa2_ref, gba_ref = (next(it) for _ in range(8))
    rcw_ref, rcb_ref, rwa_ref, rba_ref, rwx_ref, rbx_ref, rlam_ref = (next(it) for _ in range(7))
    scw_ref, ong_ref, wout_ref, hnm_ref, cs_ref, trid_ref, trik_ref = (next(it) for _ in range(7))
    st0_ref, h0_ref, rg0_ref, sc0_ref = (next(it) for _ in range(4))
    if has_past:
        pk_ref, pv_ref = next(it), next(it)
    if n_done:
        dk_ref, dv_ref = next(it), next(it)
    x1_ref, kct_ref, vct_ref, sto_ref, ho_ref, rgo_ref, sco_ref = (next(it) for _ in range(7))
    hn_scr, st_scr, h_scr, rgp_scr, scp_scr, qm_scr, acc_scr, car_scr = (next(it) for _ in range(8))
    if own_scratch:
        kt_scr, vt_scr = next(it), next(it)

    i = pl.program_id(1)

    @pl.when(i == 0)
    def _init():
        st_scr[...] = st0_ref[0]
        h_scr[...] = h0_ref[0]
        rgp_scr[...] = rg0_ref[0]
        scp_scr[...] = sc0_ref[0]

    x = x_ref[0]
    mod = mod_ref[0]
    sh1 = mod[:, 0:D_MODEL]
    sc1 = mod[:, D_MODEL:2 * D_MODEL]
    g1 = mod[:, 2 * D_MODEL:3 * D_MODEL]
    hn = ((_rms(x) * ln1_ref[...]) * (1.0 + sc1) + sh1).astype(BF16)
    if tkd > tm:
        hn = jnp.concatenate([hn, jnp.zeros((tkd - tm, D_MODEL), BF16)], axis=0)
    hn_scr[...] = hn

    def proj(first, last=None):
        a, b = SEG[first][0], SEG[last or first][1]
        return _dot(hn_scr[0:tm, :], win_ref[:, a:b])

    ong = ong_ref[...]

    def head_norm(v, group):
        ms = _dot((v * v).astype(BF16), hnm_ref[...])
        return v * (lax.rsqrt(ms + EPS) * ong[:, group * GROUP_W:(group + 1) * GROUP_W])


    xb = proj("xb")
    rgp = rgp_scr[...]
    rcw = rcw_ref[...]
    xcv = _shift_rows(xb, rgp, 3) * rcw[0:1]
    xcv = xcv + _shift_rows(xb, rgp, 2) * rcw[1:2]
    xcv = xcv + _shift_rows(xb, rgp, 1) * rcw[2:3]
    xcv = xcv + xb * rcw[3:4]
    xcv = xcv + rcb_ref[...]
    rgp_scr[...] = xb[tm - SUBLANE:tm]
    q = proj("qc") * (HEAD_DIM ** -0.5 * LOG2E)
    gb = proj("gb")
    xcvb = xcv.astype(BF16)
    r_gate = _sigmoid(_dot(xcvb, rwa_ref[...]) + rba_ref[...])
    i_gate = _sigmoid(_dot(xcvb, rwx_ref[...]) + rbx_ref[...])
    kt = _dot_nt(wkt_ref[...], hn_scr[...])
    vt = _dot_nt(wvt_ref[...], hn_scr[...])
    log_a = (-RG_C) * r_gate * _softplus(-rlam_ref[...])
    a_sc = jnp.exp(log_a)
    u_sc = jnp.sqrt(1.0 - jnp.exp(2.0 * log_a)) * (i_gate * xcv)

    for l in range(n_done):
        kct_ref[l, 0] = dk_ref[l, 0]
        vct_ref[l, 0] = dv_ref[l, 0]
    kct_ref[n_done, 0] = kt[:, 0:tm]
    vct_ref[n_done, 0] = vt[:, 0:tm]
    lane_head = lax.broadcasted_iota(jnp.int32, (1, GROUP_W), 1) >> 6
    for s in range(n_sub):
        qs = q[s * ta:(s + 1) * ta]
        for h in range(N_HEADS):
            qm_scr[s, h * ta:(h + 1) * ta, :] = jnp.where(lane_head == h, qs, 0.0).astype(BF16)
    ktb = kt.astype(BF16)
    vtb = vt.astype(BF16)
    if own_scratch:
        for s in range(n_sub):
            kt_scr[i * n_sub + s] = ktb[:, s * tk:(s + 1) * tk]
            vt_scr[i * n_sub + s] = vtb[:, s * tk:(s + 1) * tk]
    acc_scr[...] = jnp.zeros_like(acc_scr)
    car_scr[...] = jnp.zeros_like(car_scr)

    hs, h_last = _linear_scan(a_sc, u_sc, h_scr[...])
    h_scr[...] = h_last
    o_b = head_norm(hs * _gelu_tanh(gb), 1)

    pending = {
        "qk": lambda: proj("qa", "ka"), "va": lambda: proj("va"), "lra": lambda: proj("lra"),
        "ra": lambda: proj("ra"), "bd": lambda: proj("bd"), "cd": lambda: proj("cd"),
        "xd": lambda: proj("xd"),
    }
    issued = {}

    def issue_next():
        if pending:
            name = next(iter(pending))
            issued[name] = pending.pop(name)()

    def projected(name):
        if name not in issued:
            issued[name] = pending.pop(name)()
        return issued[name]

    def attn_block(s, ktblk, vtblk, tri_ref, causal_mask, valid=None, between=lambda: None):
        tkb = ktblk.shape[1]
        z_all = _dot(qm_scr[s], ktblk)
        lnegs, lposs = [], []
        for h in range(N_HEADS):
            z = z_all[h * ta:(h + 1) * ta]
            if causal_mask:
                qrow = lax.broadcasted_iota(jnp.int32, (ta, tkb), 0)
                kcol = lax.broadcasted_iota(jnp.int32, (ta, tkb), 1)
                z = jnp.where(kcol < qrow, z, SB_MASKED)
            nz = -z
            l1p = jnp.log2(1.0 + jnp.exp2(jnp.minimum(z, nz)))
            lneg = jnp.minimum(nz, 0.0) - l1p
            lposs.append(lneg + z)
            lnegs.append(lneg.astype(BF16))
            between()
        rt = _dot(jnp.concatenate(lnegs, axis=0), tri_ref[...])
        car = car_scr[s]
        car_scr[s] = car + rt[:, tkb:tkb + LANE]
        vrow_head = lax.broadcasted_iota(jnp.int32, (GROUP_W, tkb), 0) >> 6
        ws, vs = [], []
        for h in range(N_HEADS):
            rows = slice(h * ta, (h + 1) * ta)
            carh = car[rows]
            carb = carh if tkb == LANE else jnp.concatenate([carh] * (tkb // LANE), axis=1)
            ws.append(jnp.exp2(lposs[h] + (rt[rows, 0:tkb] + carb)).astype(BF16))
            v_h = jnp.where(vrow_head == h, vtblk, jnp.zeros_like(vtblk))
            if valid is not None:
                v_h = jnp.where(valid, v_h, jnp.zeros_like(v_h))
            vs.append(v_h)
            between()
        acc_scr[s * ta:(s + 1) * ta, :] += _dot_nt(jnp.concatenate(ws, axis=1),
                                                  jnp.concatenate(vs, axis=1))

    def older_block(j):
        if has_past:
            c0 = pl.multiple_of(j * tk, tk)
            return (pk_ref[0, 0, :, pl.ds(c0, tk)].astype(BF16),
                    pv_ref[0, 0, :, pl.ds(c0, tk)].astype(BF16))
        return kt_scr[j], vt_scr[j]

    for s in range(n_sub):
        attn_block(s, ktb[:, s * tkd:(s + 1) * tkd], vtb[:, s * tkd:(s + 1) * tkd], trid_ref, True,
                   between=issue_next)
        if has_past or own_scratch:
            if has_past:
                kp, vp = older_block(p_len // tk - 1)
                attn_block(s, kp, vp, trik_ref, False, between=issue_next)
            elif s > 0:
                attn_block(s, ktb[:, (s - 1) * tk:s * tk], vtb[:, (s - 1) * tk:s * tk], trik_ref,
                           False, between=issue_next)
            else:
                kp, vp = older_block(jnp.maximum(i * n_sub - 1, 0))
                attn_block(s, kp, vp, trik_ref, False, valid=i >= 1, between=issue_next)

    qk = projected("qk")
    va = projected("va")
    lra = projected("lra")
    ra = projected("ra")
    qa = qk[:, 0:LANE] * (GLA_DK ** -0.5)
    ka = qk[:, LANE:2 * LANE]
    la = -_softplus(-(_dot(lra.astype(BF16), wa2_ref[...]) + gba_ref[...])) * (1.0 / GLA_TAU)
    la_hi, la_lo = _split2(la)
    gg = _dot(cs_ref[...], jnp.concatenate([la_hi, la_lo], axis=1))
    gg = gg[:, 0:LANE] + gg[:, LANE:2 * LANE]
    g_cum = gg[0:tm]
    g_end = gg[tm:2 * tm]
    kd = (ka * jnp.exp(g_end - g_cum)).astype(BF16)
    dec = jnp.exp(g_end)
    vab = va.astype(BF16)
    qab = qa.astype(BF16)
    st_row = lax.broadcasted_iota(jnp.int32, (GROUP_W, N_HEADS * GLA_DK), 0) >> 6
    st_col = lax.broadcasted_iota(jnp.int32, (GROUP_W, N_HEADS * GLA_DK), 1) >> 5
    st_mask = st_row == st_col
    n_chunks = tm // CHUNK
    uts = [_dot_tn(vab[c * CHUNK:(c + 1) * CHUNK], kd[c * CHUNK:(c + 1) * CHUNK])
           for c in range(n_chunks)]
    st = st_scr[...]
    sts = []
    for c in range(n_chunks):
        st = st * dec[c * CHUNK:c * CHUNK + 1] + jnp.where(st_mask, uts[c], 0.0)
        sts.append(st.astype(BF16))
    st_scr[...] = st
    o_chunks = [_dot_nt(qab[c * CHUNK:(c + 1) * CHUNK], sts[c]) for c in range(n_chunks)]
    o_gla = o_chunks[0] if n_chunks == 1 else jnp.concatenate(o_chunks, axis=0)
    o_a = head_norm(o_gla, 0) * (ra * _sigmoid(ra))

    bd = projected("bd")
    cx = projected("cd") * projected("xd")
    scp = scp_scr[...]
    scw = scw_ref[...]
    yd = _shift_rows(cx, scp, 2) * scw[0:1]
    yd = yd + _shift_rows(cx, scp, 1) * scw[1:2]
    yd = yd + cx * scw[2:3]
    scp_scr[...] = cx[tm - SUBLANE:tm]
    o_d = head_norm(bd * yd, 3)

    if has_past or own_scratch:
        for s in range(n_sub):
            def cond(c):
                j, m = c
                return jnp.logical_and(j >= 0, m > SB_EXIT)

            def body(c, s=s):
                j, _ = c
                kblk, vblk = older_block(j)
                attn_block(s, kblk, vblk, trik_ref, False)
                return j - 1, jnp.max(car_scr[s])

            first = (p_len // tk - 2) if has_past else (i * n_sub + s - 2)
            lax.while_loop(cond, body, (first, jnp.max(car_scr[s])))
    o_c = head_norm(acc_scr[...], 2)

    mix = _dot(o_a.astype(BF16), wout_ref[0:GROUP_W, :])
    mix = mix + _dot(o_b.astype(BF16), wout_ref[GROUP_W:2 * GROUP_W, :])
    mix = mix + _dot(o_c.astype(BF16), wout_ref[2 * GROUP_W:3 * GROUP_W, :])
    mix = mix + _dot(o_d.astype(BF16), wout_ref[3 * GROUP_W:4 * GROUP_W, :])
    x1_ref[0] = x + g1 * mix

    @pl.when(i == n_tiles - 1)
    def _fin():
        sto_ref[0] = st_scr[...]
        ho_ref[0] = h_scr[...]
        rgo_ref[0] = rgp_scr[...]
        sco_ref[0] = scp_scr[...]


def _mixer_tiles(t, p_len):
    tm = min(t, MIXER_ROWS)
    tk = KEY_BLOCK
    ta = min(tm, tk)
    tkd = max(ta, LANE)
    assert t % tm == 0 and tm % CHUNK == 0 and tm % SUBLANE == 0 and tm % ta == 0
    n_tiles = t // tm
    assert p_len == 0 or n_tiles == 1, "cached keys are only supported with a single row tile"
    assert p_len % tk == 0 and (n_tiles == 1 or ta == tk)
    return tm, ta, tk, tkd, n_tiles


def _tri_ones(n):
    s = jnp.arange(n)[:, None]
    j = jnp.arange(n)[None, :]
    return jnp.concatenate([(s > j).astype(BF16), jnp.ones((n, LANE), BF16)], axis=1)


def _cumsum_mats(tm):
    t = jnp.arange(tm)[:, None]
    s = jnp.arange(tm)[None, :]
    same = (t // CHUNK) == (s // CHUNK)
    return jnp.concatenate([(same & (s <= t)).astype(BF16), same.astype(BF16)], axis=0)


def _const_spec(shape):
    nd = len(shape)
    return pl.BlockSpec(shape, lambda b, i: (0,) * nd)


def _mixer(x, mod, w, state, past, layer, done=None):
    bsz, t, _ = x.shape
    p_len = 0 if past is None else past[0].shape[3]
    n_done = 0 if done is None else done[0].shape[0]
    tm, ta, tk, tkd, n_tiles = _mixer_tiles(t, p_len)
    tri_d = _tri_ones(tkd)
    tri_k = _tri_ones(tk)
    cs = _cumsum_mats(tm)
    per_b = lambda b, i: (b, 0, 0)
    tile = lambda b, i: (b, i, 0)
    tile_t = lambda b, i: (0, b, 0, i)

    consts = [w["ln1_g"], w["w_in"], w["wkt"], w["wvt"], w["wa2"], w["gla_ba"], w["rg_conv_w"],
              w["rg_conv_b"], w["rg_wa"], w["rg_ba"], w["rg_wx"], w["rg_bx"], w["rg_lambda"],
              w["sc_conv_w"], w["out_norm_g"], w["w_out"], w["hnm"], cs, tri_d, tri_k]
    in_specs = [pl.BlockSpec((1, tm, D_MODEL), tile), pl.BlockSpec((1, 1, N_MOD), per_b)]
    in_specs += [_const_spec(c.shape) for c in consts]
    in_specs += [pl.BlockSpec((1,) + s.shape[1:], per_b) for s in state]
    args = [x, mod] + consts + list(state)
    if past is not None:
        in_specs += [pl.BlockSpec((1, 1, GROUP_W, p_len), lambda b, i: (layer, b, 0, 0))] * 2
        args += list(past)
    if n_done:
        in_specs += [pl.BlockSpec((n_done, 1, GROUP_W, tm), tile_t)] * 2
        args += list(done)

    out_shape = [
        jax.ShapeDtypeStruct((bsz, t, D_MODEL), F32),
        jax.ShapeDtypeStruct((n_done + 1, bsz, GROUP_W, t), F32),
        jax.ShapeDtypeStruct((n_done + 1, bsz, GROUP_W, t), F32),
    ] + [jax.ShapeDtypeStruct(s.shape, F32) for s in state]
    out_specs = [pl.BlockSpec((1, tm, D_MODEL), tile),
                 pl.BlockSpec((n_done + 1, 1, GROUP_W, tm), tile_t),
                 pl.BlockSpec((n_done + 1, 1, GROUP_W, tm), tile_t)]
    out_specs += [pl.BlockSpec((1,) + s.shape[1:], per_b) for s in state]

    scratch = [
        pltpu.VMEM((max(tm, tkd), D_MODEL), BF16),
        pltpu.VMEM((GROUP_W, N_HEADS * GLA_DK), F32),
        pltpu.VMEM((1, GROUP_W), F32),
        pltpu.VMEM((SUBLANE, GROUP_W), F32),
        pltpu.VMEM((SUBLANE, GROUP_W), F32),
        pltpu.VMEM((tm // ta, N_HEADS * ta, GROUP_W), BF16),
        pltpu.VMEM((tm, GROUP_W), F32),
        pltpu.VMEM((tm // ta, N_HEADS * ta, LANE), F32),
    ]
    if n_tiles > 1:
        scratch += [pltpu.VMEM((t // tk, GROUP_W, tk), BF16)] * 2

    kern = functools.partial(_mixer_kernel, tm=tm, ta=ta, tk=tk, tkd=tkd, n_tiles=n_tiles,
                             p_len=p_len, n_done=n_done)
    return pl.pallas_call(
        kern,
        grid=(bsz, n_tiles),
        in_specs=in_specs,
        out_specs=out_specs,
        out_shape=out_shape,
        scratch_shapes=scratch,
        compiler_params=pltpu.CompilerParams(
            dimension_semantics=("arbitrary", "arbitrary"), vmem_limit_bytes=VMEM_LIMIT_BYTES),
        name="mixer_t%d_p%d_l%d" % (t, p_len, n_done),
    )(*args)


def _mlp_kernel(x_ref, mod_ref, g_ref, w1_ref, w2_ref, fg_ref, o_ref, *, final):
    x = x_ref[...]
    bb, tm, _ = x.shape
    mod = mod_ref[...]
    sh2 = mod[:, :, 3 * D_MODEL:4 * D_MODEL]
    sc2 = mod[:, :, 4 * D_MODEL:5 * D_MODEL]
    g2 = mod[:, :, 5 * D_MODEL:6 * D_MODEL]
    h2 = ((_rms(x) * g_ref[...]) * (1.0 + sc2) + sh2).astype(BF16).reshape(bb * tm, D_MODEL)
    u = jnp.maximum(_dot(h2, w1_ref[...]), 0.0)
    y = x + g2 * _dot((u * u).astype(BF16), w2_ref[...]).reshape(bb, tm, D_MODEL)
    if final:
        y = _rms(y) * fg_ref[...]
    o_ref[...] = y


def _mlp(x, mod, ln2_g, w1, w2, final_g, final):
    bsz, t, _ = x.shape
    tm = min(t, MLP_ROWS)
    bb = 1 if tm == MLP_ROWS else min(bsz, MLP_ROWS // tm)
    assert t % tm == 0 and bsz % bb == 0
    tile = lambda b, i: (b, i, 0)
    return pl.pallas_call(
        functools.partial(_mlp_kernel, final=final),
        grid=(bsz // bb, t // tm),
        in_specs=[
            pl.BlockSpec((bb, tm, D_MODEL), tile),
            pl.BlockSpec((bb, 1, N_MOD), lambda b, i: (b, 0, 0)),
            _const_spec(ln2_g.shape), _const_spec(w1.shape), _const_spec(w2.shape),
            _const_spec(final_g.shape),
        ],
        out_specs=pl.BlockSpec((bb, tm, D_MODEL), tile),
        out_shape=jax.ShapeDtypeStruct(x.shape, F32),
        compiler_params=pltpu.CompilerParams(
            dimension_semantics=("parallel", "parallel"), vmem_limit_bytes=VMEM_LIMIT_BYTES),
        name="mlp_t%d%s" % (t, "_final" if final else ""),
    )(x, mod, ln2_g, w1, w2, final_g)


def _block_diag(blocks):
    n, a, b = blocks.shape
    eye = jnp.eye(n, dtype=blocks.dtype)
    return jnp.einsum("nij,nm->nimj", blocks, eye).reshape(n * a, n * b)


def _layer_weights(l, ln1_g, w_in, gla_wa2, gla_ba, rg_conv_w, rg_conv_b, rg_wa, rg_ba, rg_wx,
                   rg_bx, rg_lambda, sc_conv_w, out_norm_g, w_out):
    wi = w_in[l]
    w_in_p = jnp.concatenate(
        [wi[:, :LRA_END], jnp.zeros((D_MODEL, LANE - GLA_RANK), wi.dtype), wi[:, LRA_END:SBK_START],
         wi[:, SBV_END:]], axis=1)
    wa2 = jnp.zeros((LANE, N_HEADS * GLA_DK), F32).at[:GLA_RANK].set(gla_wa2[l])
    hnm = _block_diag(jnp.full((N_HEADS, HEAD_DIM, HEAD_DIM), 1.0 / HEAD_DIM, F32))
    row = lambda v: v[l].reshape(1, -1)
    return {
        "ln1_g": row(ln1_g), "w_in": w_in_p.astype(BF16),
        "wkt": wi[:, SBK_START:SBV_START].T.astype(BF16), "wvt": wi[:, SBV_START:SBV_END].T.astype(BF16),
        "wa2": wa2.astype(BF16),
        "gla_ba": row(gla_ba), "rg_conv_w": rg_conv_w[l], "rg_conv_b": row(rg_conv_b),
        "rg_wa": _block_diag(rg_wa[l]).astype(BF16), "rg_ba": row(rg_ba),
        "rg_wx": _block_diag(rg_wx[l]).astype(BF16), "rg_bx": row(rg_bx),
        "rg_lambda": row(rg_lambda), "sc_conv_w": sc_conv_w[l], "out_norm_g": row(out_norm_g),
        "w_out": w_out[l].astype(BF16), "hnm": hnm.astype(BF16),
    }


def _state_in(s_gla, h_rg, buf_rg, buf_sc):
    bsz = s_gla.shape[0]
    eye = jnp.eye(N_HEADS, dtype=F32)
    st = jnp.einsum("bhkv,hg->bhvgk", s_gla.astype(F32), eye).reshape(bsz, GROUP_W, N_HEADS * GLA_DK)
    pad8 = lambda buf: jnp.concatenate(
        [jnp.zeros((bsz, SUBLANE - buf.shape[1], GROUP_W), F32), buf.astype(F32)], axis=1)
    return st, h_rg.astype(F32).reshape(bsz, 1, GROUP_W), pad8(buf_rg), pad8(buf_sc)


def _state_out(st, h, rgbuf, scbuf):
    bsz = st.shape[0]
    st5 = st.reshape(bsz, N_HEADS, HEAD_DIM, N_HEADS, GLA_DK)
    s_gla = jnp.stack([st5[:, h_, :, h_, :] for h_ in range(N_HEADS)], axis=1)
    return (jnp.swapaxes(s_gla, 2, 3), h.reshape(bsz, GROUP_W),
            rgbuf[:, SUBLANE - (RG_CONV - 1):], scbuf[:, SUBLANE - (SC_CONV - 1):])


def _rows_from_transposed(xt):
    l, b, _, t = xt.shape
    return jnp.transpose(xt.reshape(l, b, N_HEADS, HEAD_DIM, t), (0, 1, 4, 2, 3))


def kernel(x_prompt, x_sample, c_prompt, c_sample, state_gla, state_rg_h, state_rg_conv, cache_sb_k, cache_sb_v, state_sc_conv, ln1_g, ln2_g, w_ada, b_ada, w_in, gla_wa2, gla_ba, rg_conv_w, rg_conv_b, rg_wa, rg_ba, rg_wx, rg_bx, rg_lambda, sc_conv_w, out_norm_g, w_out, mlp_w1, mlp_w2, final_g):
    depth = w_in.shape[0]
    bp, bs = x_prompt.shape[0], x_sample.shape[0]
    p_len = cache_sb_k.shape[2]

    rows = -(-(bp + bs) // 16) * 16
    c_all = jnp.concatenate(
        [c_prompt, c_sample, jnp.zeros((rows - bp - bs, D_MODEL), F32)], axis=0)
    mod_all = _modulation(c_all, w_ada, b_ada)

    zero_state = _state_in(jnp.zeros((bp, N_HEADS, GLA_DK, HEAD_DIM), F32), jnp.zeros((bp, GROUP_W), F32),
                           jnp.zeros((bp, RG_CONV - 1, GROUP_W), F32), jnp.zeros((bp, SC_CONV - 1, GROUP_W), F32))
    fg = final_g.reshape(1, D_MODEL)
    past = tuple(jnp.transpose(c, (0, 1, 3, 4, 2)).reshape(depth, bs, GROUP_W, p_len)
                 for c in (cache_sb_k, cache_sb_v))

    xp, xs = x_prompt, x_sample
    kv_p, kv_s = None, None
    states_p, states_s = [], []
    for l in range(depth):
        w = _layer_weights(l, ln1_g, w_in, gla_wa2, gla_ba, rg_conv_w, rg_conv_b, rg_wa, rg_ba,
                           rg_wx, rg_bx, rg_lambda, sc_conv_w, out_norm_g, w_out)
        w1 = mlp_w1[l].astype(BF16)
        w2 = mlp_w2[l].astype(BF16)
        g2 = ln2_g[l].reshape(1, D_MODEL)
        mod_p = mod_all[l, :bp].reshape(bp, 1, N_MOD)
        mod_s = mod_all[l, bp:bp + bs].reshape(bs, 1, N_MOD)
        final = l == depth - 1

        rp = _mixer(xp, mod_p, w, zero_state, None, l, kv_p)
        sample_state = _state_in(state_gla[l], state_rg_h[l], state_rg_conv[l], state_sc_conv[l])
        rs = _mixer(xs, mod_s, w, sample_state, past, l, kv_s)
        kv_p, kv_s = (rp[1], rp[2]), (rs[1], rs[2])

        xp = _mlp(rp[0], mod_p, g2, w1, w2, fg, final)
        xs = _mlp(rs[0], mod_s, g2, w1, w2, fg, final)
        states_p.append(_state_out(*rp[3:7]))
        states_s.append(_state_out(*rs[3:7]))

    def stk(lst, j):
        return jnp.stack([st[j] for st in lst], axis=0)

    res = [xp, xs]
    for j in range(3):
        res += [stk(states_p, j), stk(states_s, j)]
    res += [_rows_from_transposed(kv_p[0]), _rows_from_transposed(kv_s[0]),
            _rows_from_transposed(kv_p[1]), _rows_from_transposed(kv_s[1]),
            stk(states_p, 3), stk(states_s, 3)]
    return tuple(res)
```

```python
import functools

import jax
import jax.numpy as jnp
from jax import lax
from jax.experimental import pallas as pl
from jax.experimental.pallas import tpu as pltpu

F32 = jnp.float32
BF16 = jnp.bfloat16

D_MODEL = 1024
GROUP_W = 256
HEAD_DIM = 64
N_HEADS = GROUP_W // HEAD_DIM
CHUNK = 64
EPS = 1e-6
GLA_DK = 32
GLA_RANK = 16
GLA_TAU = 16.0
RG_C = 8.0
RG_CONV = 4
SC_CONV = 3
D_FF = 4 * D_MODEL
N_MOD = 6 * D_MODEL

LANE = 128
SUBLANE = 8
VMEM_LIMIT_BYTES = 56 * 1024 * 1024
MIXER_ROWS = 512
MLP_ROWS = 512
KEY_BLOCK = 256

SEG = {
    "qa": (0, 128), "ka": (128, 256), "va": (256, 512), "lra": (512, 640), "ra": (640, 896),
    "gb": (896, 1152), "xb": (1152, 1408), "qc": (1408, 1664),
    "bd": (1664, 1920), "cd": (1920, 2176), "xd": (2176, 2432),
}
N_IN_PAD = 2432
LRA_END = 2 * N_HEADS * GLA_DK + N_HEADS * HEAD_DIM + GLA_RANK
SBK_START = LRA_END + 4 * GROUP_W
SBV_START = SBK_START + GROUP_W
SBV_END = SBV_START + GROUP_W

LOG2E = 1.4426950408889634
SB_EXIT = -110.0 * LOG2E
SB_MASKED = -1.0e9


def _dot(a, b):
    return jnp.dot(a, b, preferred_element_type=F32)


def _dot_nt(a, b):
    return lax.dot_general(a, b, (((1,), (1,)), ((), ())), preferred_element_type=F32)


def _dot_tn(a, b):
    return lax.dot_general(a, b, (((0,), (0,)), ((), ())), preferred_element_type=F32)


def _rms(x):
    return x * lax.rsqrt(jnp.mean(x * x, axis=-1, keepdims=True) + EPS)


def _sigmoid(x):
    return 1.0 / (1.0 + jnp.exp(-x))


def _softplus(x):
    return jnp.maximum(x, 0.0) + jnp.log(1.0 + jnp.exp(-jnp.abs(x)))


def _gelu_tanh(x):
    c = 0.7978845608028654
    return x * (0.5 * (1.0 + jnp.tanh(c * (x + 0.044715 * (x * x * x)))))


def _split2(x):
    hi = x.astype(BF16)
    lo = (x - hi.astype(F32)).astype(BF16)
    return hi, lo


def _shift_rows(x, prev8, s):
    conc = jnp.concatenate([prev8, x], axis=0)
    return pltpu.roll(conc, s, 0)[SUBLANE:, :]


def _linear_scan(a, u, h0):
    rows, cols = a.shape
    a = a.reshape(rows // SUBLANE, SUBLANE, cols)
    u = u.reshape(rows // SUBLANE, SUBLANE, cols)
    row8 = lax.broadcasted_iota(jnp.int32, a.shape, 1)
    s = 1
    while s < SUBLANE:
        keep = row8 >= s
        a_sh = jnp.where(keep, pltpu.roll(a, s, 1), 1.0)
        u_sh = jnp.where(keep, pltpu.roll(u, s, 1), 0.0)
        u = a * u_sh + u
        a = a * a_sh
        s *= 2
    a = a.reshape(rows, cols)
    u = u.reshape(rows, cols)
    h = h0
    out = []
    for g in range(rows // SUBLANE):
        sl = slice(g * SUBLANE, (g + 1) * SUBLANE)
        hs = u[sl] + a[sl] * h
        out.append(hs)
        h = hs[SUBLANE - 1:SUBLANE]
    return jnp.concatenate(out, axis=0), h


def _mod_kernel(c_ref, w_ref, b_ref, o_ref):
    c = c_ref[...]
    s = (c * _sigmoid(c)).astype(BF16)
    o_ref[0] = _dot(s, w_ref[0].astype(BF16)) + b_ref[0]


def _modulation(c_all, w_ada, b_ada):
    depth = w_ada.shape[0]
    rows = c_all.shape[0]
    tn = 1024
    return pl.pallas_call(
        _mod_kernel,
        grid=(depth, N_MOD // tn),
        in_specs=[
            pl.BlockSpec((rows, D_MODEL), lambda l, j: (0, 0)),
            pl.BlockSpec((1, D_MODEL, tn), lambda l, j: (l, 0, j)),
            pl.BlockSpec((1, 1, tn), lambda l, j: (l, 0, j)),
        ],
        out_specs=pl.BlockSpec((1, rows, tn), lambda l, j: (l, 0, j)),
        out_shape=jax.ShapeDtypeStruct((depth, rows, N_MOD), F32),
        compiler_params=pltpu.CompilerParams(
            dimension_semantics=("arbitrary", "arbitrary"), vmem_limit_bytes=VMEM_LIMIT_BYTES),
        name="adaln_modulation",
    )(c_all, w_ada, b_ada.reshape(depth, 1, N_MOD))


def _mixer_kernel(*refs, tm, ta, tk, tkd, n_tiles, p_len, n_done):
    has_past = p_len > 0
    own_scratch = n_tiles > 1
    n_sub = tm // ta
    it = iter(refs)
    x_ref, mod_ref, ln1_ref, win_ref, wkt_ref, wvt_ref, wa2_ref, gba_ref = (next(it) for _ in range(8))
    rcw_ref, rcb_ref, rwa_ref, rba_ref, rwx_ref, rbx_ref, rlam_ref = (next(it) for _ in range(7))
    scw_ref, ong_ref, wout_ref, hnm_ref, cs_ref, trid_ref, trik_ref = (next(it) for _ in range(7))
    st0_ref, h0_ref, rg0_ref, sc0_ref = (next(it) for _ in range(4))
    if has_past:
        pk_ref, pv_ref = next(it), next(it)
    if n_done:
        dk_ref, dv_ref = next(it), next(it)
    x1_ref, kct_ref, vct_ref, sto_ref, ho_ref, rgo_ref, sco_ref = (next(it) for _ in range(7))
    hn_scr, st_scr, h_scr, rgp_scr, scp_scr, qm_scr, acc_scr, car_scr = (next(it) for _ in range(8))
    if own_scratch:
        kt_scr, vt_scr = next(it), next(it)

    i = pl.program_id(1)

    @pl.when(i == 0)
    def _init():
        st_scr[...] = st0_ref[0]
        h_scr[...] = h0_ref[0]
        rgp_scr[...] = rg0_ref[0]
        scp_scr[...] = sc0_ref[0]

    x = x_ref[0]
    mod = mod_ref[0]
    sh1 = mod[:, 0:D_MODEL]
    sc1 = mod[:, D_MODEL:2 * D_MODEL]
    g1 = mod[:, 2 * D_MODEL:3 * D_MODEL]
    hn = ((_rms(x) * ln1_ref[...]) * (1.0 + sc1) + sh1).astype(BF16)
    if tkd > tm:
        hn = jnp.concatenate([hn, jnp.zeros((tkd - tm, D_MODEL), BF16)], axis=0)
    hn_scr[...] = hn

    def proj(first, last=None):
        a, b = SEG[first][0], SEG[last or first][1]
        return _dot(hn_scr[0:tm, :], win_ref[:, a:b])

    ong = ong_ref[...]

    def head_norm(v, group):
        ms = _dot((v * v).astype(BF16), hnm_ref[...])
        return v * (lax.rsqrt(ms + EPS) * ong[:, group * GROUP_W:(group + 1) * GROUP_W])


    xb = proj("xb")
    rgp = rgp_scr[...]
    rcw = rcw_ref[...]
    xcv = _shift_rows(xb, rgp, 3) * rcw[0:1]
    xcv = xcv + _shift_rows(xb, rgp, 2) * rcw[1:2]
    xcv = xcv + _shift_rows(xb, rgp, 1) * rcw[2:3]
    xcv = xcv + xb * rcw[3:4]
    xcv = xcv + rcb_ref[...]
    rgp_scr[...] = xb[tm - SUBLANE:tm]
    q = proj("qc") * (HEAD_DIM ** -0.5 * LOG2E)
    gb = proj("gb")
    xcvb = xcv.astype(BF16)
    r_gate = _sigmoid(_dot(xcvb, rwa_ref[...]) + rba_ref[...])
    i_gate = _sigmoid(_dot(xcvb, rwx_ref[...]) + rbx_ref[...])
    kt = _dot_nt(wkt_ref[...], hn_scr[...])
    vt = _dot_nt(wvt_ref[...], hn_scr[...])
    log_a = (-RG_C) * r_gate * _softplus(-rlam_ref[...])
    a_sc = jnp.exp(log_a)
    u_sc = jnp.sqrt(-jnp.tanh(log_a) * (jnp.exp(2.0 * log_a) + 1.0)) * (i_gate * xcv)

    for l in range(n_done):
        kct_ref[l, 0] = dk_ref[l, 0]
        vct_ref[l, 0] = dv_ref[l, 0]
    kct_ref[n_done, 0] = kt[:, 0:tm]
    vct_ref[n_done, 0] = vt[:, 0:tm]
    lane_head = lax.broadcasted_iota(jnp.int32, (1, GROUP_W), 1) >> 6
    for s in range(n_sub):
        qs = q[s * ta:(s + 1) * ta]
        for h in range(N_HEADS):
            qm_scr[s, h * ta:(h + 1) * ta, :] = jnp.where(lane_head == h, qs, 0.0).astype(BF16)
    ktb = kt.astype(BF16)
    vtb = vt.astype(BF16)
    if own_scratch:
        for s in range(n_sub):
            kt_scr[i * n_sub + s] = ktb[:, s * tk:(s + 1) * tk]
            vt_scr[i * n_sub + s] = vtb[:, s * tk:(s + 1) * tk]
    acc_scr[...] = jnp.zeros_like(acc_scr)
    car_scr[...] = jnp.zeros_like(car_scr)

    hs, h_last = _linear_scan(a_sc, u_sc, h_scr[...])
    h_scr[...] = h_last
    o_b = head_norm(hs * _gelu_tanh(gb), 1)

    pending = {
        "qk": lambda: proj("qa", "ka"), "va": lambda: proj("va"), "lrra": lambda: proj("lra", "ra"),
        "bd": lambda: proj("bd"), "cd": lambda: proj("cd"), "xd": lambda: proj("xd"),
    }
    issued = {}

    def issue_next():
        if pending:
            name = next(iter(pending))
            issued[name] = pending.pop(name)()

    def projected(name):
        if name not in issued:
            issued[name] = pending.pop(name)()
        return issued[name]

    def attn_block(s, ktblk, vtblk, tri_ref, causal_mask, valid=None, between=lambda: None):
        tkb = ktblk.shape[1]
        lnegs, lposs = [], []
        for h in range(N_HEADS):
            z = _dot(qm_scr[s, h * ta:(h + 1) * ta, :], ktblk)
            if causal_mask:
                qrow = lax.broadcasted_iota(jnp.int32, (ta, tkb), 0)
                kcol = lax.broadcasted_iota(jnp.int32, (ta, tkb), 1)
                z = jnp.where(kcol < qrow, z, SB_MASKED)
            nz = -z
            l1p = jnp.log2(1.0 + jnp.exp2(jnp.minimum(z, nz)))
            lneg = jnp.minimum(nz, 0.0) - l1p
            lposs.append(lneg + z)
            lnegs.append(lneg.astype(BF16))
            between()
        rt = _dot(jnp.concatenate(lnegs, axis=0), tri_ref[...])
        car = car_scr[s]
        car_scr[s] = car + rt[:, tkb:tkb + LANE]
        vrow_head = lax.broadcasted_iota(jnp.int32, (GROUP_W, tkb), 0) >> 6
        ws, vs = [], []
        for h in range(N_HEADS):
            rows = slice(h * ta, (h + 1) * ta)
            carh = car[rows]
            carb = carh if tkb == LANE else jnp.concatenate([carh] * (tkb // LANE), axis=1)
            ws.append(jnp.exp2(lposs[h] + (rt[rows, 0:tkb] + carb)).astype(BF16))
            v_h = jnp.where(vrow_head == h, vtblk, jnp.zeros_like(vtblk))
            if valid is not None:
                v_h = jnp.where(valid, v_h, jnp.zeros_like(v_h))
            vs.append(v_h)
            between()
        acc_scr[s * ta:(s + 1) * ta, :] += _dot_nt(jnp.concatenate(ws, axis=1),
                                                  jnp.concatenate(vs, axis=1))

    def older_block(j):
        if has_past:
            c0 = pl.multiple_of(j * tk, tk)
            return (pk_ref[0, 0, :, pl.ds(c0, tk)].astype(BF16),
                    pv_ref[0, 0, :, pl.ds(c0, tk)].astype(BF16))
        return kt_scr[j], vt_scr[j]

    for s in range(n_sub):
        attn_block(s, ktb[:, s * tkd:(s + 1) * tkd], vtb[:, s * tkd:(s + 1) * tkd], trid_ref, True,
                   between=issue_next)
        if has_past or own_scratch:
            if has_past:
                kp, vp = older_block(p_len // tk - 1)
                attn_block(s, kp, vp, trik_ref, False, between=issue_next)
            elif s > 0:
                attn_block(s, ktb[:, (s - 1) * tk:s * tk], vtb[:, (s - 1) * tk:s * tk], trik_ref,
                           False, between=issue_next)
            else:
                kp, vp = older_block(jnp.maximum(i * n_sub - 1, 0))
                attn_block(s, kp, vp, trik_ref, False, valid=i >= 1, between=issue_next)

    qk = projected("qk")
    va = projected("va")
    lrra = projected("lrra")
    lra = lrra[:, 0:LANE]
    ra = lrra[:, LANE:LANE + GROUP_W]
    qa = qk[:, 0:LANE] * (GLA_DK ** -0.5)
    ka = qk[:, LANE:2 * LANE]
    la = -_softplus(-(_dot(lra.astype(BF16), wa2_ref[...]) + gba_ref[...])) * (1.0 / GLA_TAU)
    la_hi, la_lo = _split2(la)
    gg = _dot(cs_ref[...], jnp.concatenate([la_hi, la_lo], axis=1))
    gg = gg[:, 0:LANE] + gg[:, LANE:2 * LANE]
    g_cum = gg[0:tm]
    g_end = gg[tm:2 * tm]
    kd = (ka * jnp.exp(g_end - g_cum)).astype(BF16)
    dec = jnp.exp(g_end)
    vab = va.astype(BF16)
    qab = qa.astype(BF16)
    st_row = lax.broadcasted_iota(jnp.int32, (GROUP_W, N_HEADS * GLA_DK), 0) >> 6
    st_col = lax.broadcasted_iota(jnp.int32, (GROUP_W, N_HEADS * GLA_DK), 1) >> 5
    st_mask = st_row == st_col
    n_chunks = tm // CHUNK
    uts = [_dot_tn(vab[c * CHUNK:(c + 1) * CHUNK], kd[c * CHUNK:(c + 1) * CHUNK])
           for c in range(n_chunks)]
    st = st_scr[...]
    sts = []
    for c in range(n_chunks):
        st = st * dec[c * CHUNK:c * CHUNK + 1] + jnp.where(st_mask, uts[c], 0.0)
        sts.append(st.astype(BF16))
    st_scr[...] = st
    o_chunks = [_dot_nt(qab[c * CHUNK:(c + 1) * CHUNK], sts[c]) for c in range(n_chunks)]
    o_gla = o_chunks[0] if n_chunks == 1 else jnp.concatenate(o_chunks, axis=0)
    o_a = head_norm(o_gla, 0) * (ra * _sigmoid(ra))

    bd = projected("bd")
    cx = projected("cd") * projected("xd")
    scp = scp_scr[...]
    scw = scw_ref[...]
    yd = _shift_rows(cx, scp, 2) * scw[0:1]
    yd = yd + _shift_rows(cx, scp, 1) * scw[1:2]
    yd = yd + cx * scw[2:3]
    scp_scr[...] = cx[tm - SUBLANE:tm]
    o_d = head_norm(bd * yd, 3)

    if has_past or own_scratch:
        for s in range(n_sub):
            def cond(c):
                j, m = c
                return jnp.logical_and(j >= 0, m > SB_EXIT)

            def body(c, s=s):
                j, _ = c
                kblk, vblk = older_block(j)
                attn_block(s, kblk, vblk, trik_ref, False)
                return j - 1, jnp.max(car_scr[s])

            first = (p_len // tk - 2) if has_past else (i * n_sub + s - 2)
            lax.while_loop(cond, body, (first, jnp.max(car_scr[s])))
    o_c = head_norm(acc_scr[...], 2)

    mix = _dot(o_a.astype(BF16), wout_ref[0:GROUP_W, :])
    mix = mix + _dot(o_b.astype(BF16), wout_ref[GROUP_W:2 * GROUP_W, :])
    mix = mix + _dot(o_c.astype(BF16), wout_ref[2 * GROUP_W:3 * GROUP_W, :])
    mix = mix + _dot(o_d.astype(BF16), wout_ref[3 * GROUP_W:4 * GROUP_W, :])
    x1_ref[0] = x + g1 * mix

    @pl.when(i == n_tiles - 1)
    def _fin():
        sto_ref[0] = st_scr[...]
        ho_ref[0] = h_scr[...]
        rgo_ref[0] = rgp_scr[...]
        sco_ref[0] = scp_scr[...]


def _mixer_tiles(t, p_len):
    tm = min(t, MIXER_ROWS)
    tk = KEY_BLOCK
    ta = min(tm, tk)
    tkd = max(ta, LANE)
    assert t % tm == 0 and tm % CHUNK == 0 and tm % SUBLANE == 0 and tm % ta == 0
    n_tiles = t // tm
    assert p_len == 0 or n_tiles == 1, "cached keys are only supported with a single row tile"
    assert p_len % tk == 0 and (n_tiles == 1 or ta == tk)
    return tm, ta, tk, tkd, n_tiles


def _tri_ones(n):
    s = jnp.arange(n)[:, None]
    j = jnp.arange(n)[None, :]
    return jnp.concatenate([(s > j).astype(BF16), jnp.ones((n, LANE), BF16)], axis=1)


def _cumsum_mats(tm):
    t = jnp.arange(tm)[:, None]
    s = jnp.arange(tm)[None, :]
    same = (t // CHUNK) == (s // CHUNK)
    return jnp.concatenate([(same & (s <= t)).astype(BF16), same.astype(BF16)], axis=0)


def _const_spec(shape):
    nd = len(shape)
    return pl.BlockSpec(shape, lambda b, i: (0,) * nd)


def _mixer(x, mod, w, state, past, layer, done=None):
    bsz, t, _ = x.shape
    p_len = 0 if past is None else past[0].shape[3]
    n_done = 0 if done is None else done[0].shape[0]
    tm, ta, tk, tkd, n_tiles = _mixer_tiles(t, p_len)
    tri_d = _tri_ones(tkd)
    tri_k = _tri_ones(tk)
    cs = _cumsum_mats(tm)
    per_b = lambda b, i: (b, 0, 0)
    tile = lambda b, i: (b, i, 0)
    tile_t = lambda b, i: (0, b, 0, i)

    consts = [w["ln1_g"], w["w_in"], w["wkt"], w["wvt"], w["wa2"], w["gla_ba"], w["rg_conv_w"],
              w["rg_conv_b"], w["rg_wa"], w["rg_ba"], w["rg_wx"], w["rg_bx"], w["rg_lambda"],
              w["sc_conv_w"], w["out_norm_g"], w["w_out"], w["hnm"], cs, tri_d, tri_k]
    in_specs = [pl.BlockSpec((1, tm, D_MODEL), tile), pl.BlockSpec((1, 1, N_MOD), per_b)]
    in_specs += [_const_spec(c.shape) for c in consts]
    in_specs += [pl.BlockSpec((1,) + s.shape[1:], per_b) for s in state]
    args = [x, mod] + consts + list(state)
    if past is not None:
        in_specs += [pl.BlockSpec((1, 1, GROUP_W, p_len), lambda b, i: (layer, b, 0, 0))] * 2
        args += list(past)
    if n_done:
        in_specs += [pl.BlockSpec((n_done, 1, GROUP_W, tm), tile_t)] * 2
        args += list(done)

    out_shape = [
        jax.ShapeDtypeStruct((bsz, t, D_MODEL), F32),
        jax.ShapeDtypeStruct((n_done + 1, bsz, GROUP_W, t), F32),
        jax.ShapeDtypeStruct((n_done + 1, bsz, GROUP_W, t), F32),
    ] + [jax.ShapeDtypeStruct(s.shape, F32) for s in state]
    out_specs = [pl.BlockSpec((1, tm, D_MODEL), tile),
                 pl.BlockSpec((n_done + 1, 1, GROUP_W, tm), tile_t),
                 pl.BlockSpec((n_done + 1, 1, GROUP_W, tm), tile_t)]
    out_specs += [pl.BlockSpec((1,) + s.shape[1:], per_b) for s in state]

    scratch = [
        pltpu.VMEM((max(tm, tkd), D_MODEL), BF16),
        pltpu.VMEM((GROUP_W, N_HEADS * GLA_DK), F32),
        pltpu.VMEM((1, GROUP_W), F32),
        pltpu.VMEM((SUBLANE, GROUP_W), F32),
        pltpu.VMEM((SUBLANE, GROUP_W), F32),
        pltpu.VMEM((tm // ta, N_HEADS * ta, GROUP_W), BF16),
        pltpu.VMEM((tm, GROUP_W), F32),
        pltpu.VMEM((tm // ta, N_HEADS * ta, LANE), F32),
    ]
    if n_tiles > 1:
        scratch += [pltpu.VMEM((t // tk, GROUP_W, tk), BF16)] * 2

    kern = functools.partial(_mixer_kernel, tm=tm, ta=ta, tk=tk, tkd=tkd, n_tiles=n_tiles,
                             p_len=p_len, n_done=n_done)
    return pl.pallas_call(
        kern,
        grid=(bsz, n_tiles),
        in_specs=in_specs,
        out_specs=out_specs,
        out_shape=out_shape,
        scratch_shapes=scratch,
        compiler_params=pltpu.CompilerParams(
            dimension_semantics=("arbitrary", "arbitrary"), vmem_limit_bytes=VMEM_LIMIT_BYTES),
        name="mixer_t%d_p%d_l%d" % (t, p_len, n_done),
    )(*args)


def _mlp_kernel(x_ref, mod_ref, g_ref, w1_ref, w2_ref, fg_ref, o_ref, *, final):
    x = x_ref[...]
    bb, tm, _ = x.shape
    mod = mod_ref[...]
    sh2 = mod[:, :, 3 * D_MODEL:4 * D_MODEL]
    sc2 = mod[:, :, 4 * D_MODEL:5 * D_MODEL]
    g2 = mod[:, :, 5 * D_MODEL:6 * D_MODEL]
    h2 = ((_rms(x) * g_ref[...]) * (1.0 + sc2) + sh2).astype(BF16).reshape(bb * tm, D_MODEL)
    u = jnp.maximum(_dot(h2, w1_ref[...]), 0.0)
    y = x + g2 * _dot((u * u).astype(BF16), w2_ref[...]).reshape(bb, tm, D_MODEL)
    if final:
        y = _rms(y) * fg_ref[...]
    o_ref[...] = y


def _mlp(x, mod, ln2_g, w1, w2, final_g, final):
    bsz, t, _ = x.shape
    tm = min(t, MLP_ROWS)
    bb = 1 if tm == MLP_ROWS else min(bsz, MLP_ROWS // tm)
    assert t % tm == 0 and bsz % bb == 0
    tile = lambda b, i: (b, i, 0)
    return pl.pallas_call(
        functools.partial(_mlp_kernel, final=final),
        grid=(bsz // bb, t // tm),
        in_specs=[
            pl.BlockSpec((bb, tm, D_MODEL), tile),
            pl.BlockSpec((bb, 1, N_MOD), lambda b, i: (b, 0, 0)),
            _const_spec(ln2_g.shape), _const_spec(w1.shape), _const_spec(w2.shape),
            _const_spec(final_g.shape),
        ],
        out_specs=pl.BlockSpec((bb, tm, D_MODEL), tile),
        out_shape=jax.ShapeDtypeStruct(x.shape, F32),
        compiler_params=pltpu.CompilerParams(
            dimension_semantics=("parallel", "parallel"), vmem_limit_bytes=VMEM_LIMIT_BYTES),
        name="mlp_t%d%s" % (t, "_final" if final else ""),
    )(x, mod, ln2_g, w1, w2, final_g)


def _block_diag(blocks):
    n, a, b = blocks.shape
    eye = jnp.eye(n, dtype=blocks.dtype)
    return jnp.einsum("nij,nm->nimj", blocks, eye).reshape(n * a, n * b)


def _layer_weights(l, ln1_g, w_in, gla_wa2, gla_ba, rg_conv_w, rg_conv_b, rg_wa, rg_ba, rg_wx,
                   rg_bx, rg_lambda, sc_conv_w, out_norm_g, w_out):
    wi = w_in[l]
    w_in_p = jnp.concatenate(
        [wi[:, :LRA_END], jnp.zeros((D_MODEL, LANE - GLA_RANK), wi.dtype), wi[:, LRA_END:SBK_START],
         wi[:, SBV_END:]], axis=1)
    wa2 = jnp.zeros((LANE, N_HEADS * GLA_DK), F32).at[:GLA_RANK].set(gla_wa2[l])
    hnm = _block_diag(jnp.full((N_HEADS, HEAD_DIM, HEAD_DIM), 1.0 / HEAD_DIM, F32))
    row = lambda v: v[l].reshape(1, -1)
    return {
        "ln1_g": row(ln1_g), "w_in": w_in_p.astype(BF16),
        "wkt": wi[:, SBK_START:SBV_START].T.astype(BF16), "wvt": wi[:, SBV_START:SBV_END].T.astype(BF16),
        "wa2": wa2.astype(BF16),
        "gla_ba": row(gla_ba), "rg_conv_w": rg_conv_w[l], "rg_conv_b": row(rg_conv_b),
        "rg_wa": _block_diag(rg_wa[l]).astype(BF16), "rg_ba": row(rg_ba),
        "rg_wx": _block_diag(rg_wx[l]).astype(BF16), "rg_bx": row(rg_bx),
        "rg_lambda": row(rg_lambda), "sc_conv_w": sc_conv_w[l], "out_norm_g": row(out_norm_g),
        "w_out": w_out[l].astype(BF16), "hnm": hnm.astype(BF16),
    }


def _state_in(s_gla, h_rg, buf_rg, buf_sc):
    bsz = s_gla.shape[0]
    eye = jnp.eye(N_HEADS, dtype=F32)
    st = jnp.einsum("bhkv,hg->bhvgk", s_gla.astype(F32), eye).reshape(bsz, GROUP_W, N_HEADS * GLA_DK)
    pad8 = lambda buf: jnp.concatenate(
        [jnp.zeros((bsz, SUBLANE - buf.shape[1], GROUP_W), F32), buf.astype(F32)], axis=1)
    return st, h_rg.astype(F32).reshape(bsz, 1, GROUP_W), pad8(buf_rg), pad8(buf_sc)


def _state_out(st, h, rgbuf, scbuf):
    bsz = st.shape[0]
    st5 = st.reshape(bsz, N_HEADS, HEAD_DIM, N_HEADS, GLA_DK)
    s_gla = jnp.stack([st5[:, h_, :, h_, :] for h_ in range(N_HEADS)], axis=1)
    return (jnp.swapaxes(s_gla, 2, 3), h.reshape(bsz, GROUP_W),
            rgbuf[:, SUBLANE - (RG_CONV - 1):], scbuf[:, SUBLANE - (SC_CONV - 1):])


def _rows_from_transposed(xt):
    l, b, _, t = xt.shape
    return jnp.transpose(xt.reshape(l, b, N_HEADS, HEAD_DIM, t), (0, 1, 4, 2, 3))


def kernel(x_prompt, x_sample, c_prompt, c_sample, state_gla, state_rg_h, state_rg_conv, cache_sb_k, cache_sb_v, state_sc_conv, ln1_g, ln2_g, w_ada, b_ada, w_in, gla_wa2, gla_ba, rg_conv_w, rg_conv_b, rg_wa, rg_ba, rg_wx, rg_bx, rg_lambda, sc_conv_w, out_norm_g, w_out, mlp_w1, mlp_w2, final_g):
    depth = w_in.shape[0]
    bp, bs = x_prompt.shape[0], x_sample.shape[0]
    p_len = cache_sb_k.shape[2]

    rows = -(-(bp + bs) // 16) * 16
    c_all = jnp.concatenate(
        [c_prompt, c_sample, jnp.zeros((rows - bp - bs, D_MODEL), F32)], axis=0)
    mod_all = _modulation(c_all, w_ada, b_ada)

    zero_state = _state_in(jnp.zeros((bp, N_HEADS, GLA_DK, HEAD_DIM), F32), jnp.zeros((bp, GROUP_W), F32),
                           jnp.zeros((bp, RG_CONV - 1, GROUP_W), F32), jnp.zeros((bp, SC_CONV - 1, GROUP_W), F32))
    fg = final_g.reshape(1, D_MODEL)
    past = tuple(jnp.transpose(c, (0, 1, 3, 4, 2)).reshape(depth, bs, GROUP_W, p_len)
                 for c in (cache_sb_k, cache_sb_v))

    xp, xs = x_prompt, x_sample
    kv_p, kv_s = None, None
    states_p, states_s = [], []
    for l in range(depth):
        w = _layer_weights(l, ln1_g, w_in, gla_wa2, gla_ba, rg_conv_w, rg_conv_b, rg_wa, rg_ba,
                           rg_wx, rg_bx, rg_lambda, sc_conv_w, out_norm_g, w_out)
        w1 = mlp_w1[l].astype(BF16)
        w2 = mlp_w2[l].astype(BF16)
        g2 = ln2_g[l].reshape(1, D_MODEL)
        mod_p = mod_all[l, :bp].reshape(bp, 1, N_MOD)
        mod_s = mod_all[l, bp:bp + bs].reshape(bs, 1, N_MOD)
        final = l == depth - 1

        rp = _mixer(xp, mod_p, w, zero_state, None, l, kv_p)
        sample_state = _state_in(state_gla[l], state_rg_h[l], state_rg_conv[l], state_sc_conv[l])
        rs = _mixer(xs, mod_s, w, sample_state, past, l, kv_s)
        kv_p, kv_s = (rp[1], rp[2]), (rs[1], rs[2])

        xp = _mlp(rp[0], mod_p, g2, w1, w2, fg, final)
        xs = _mlp(rs[0], mod_s, g2, w1, w2, fg, final)
        states_p.append(_state_out(*rp[3:7]))
        states_s.append(_state_out(*rs[3:7]))

    def stk(lst, j):
        return jnp.stack([st[j] for st in lst], axis=0)

    res = [xp, xs]
    for j in range(3):
        res += [stk(states_p, j), stk(states_s, j)]
    res += [_rows_from_transposed(kv_p[0]), _rows_from_transposed(kv_s[0]),
            _rows_from_transposed(kv_p[1]), _rows_from_transposed(kv_s[1]),
            stk(states_p, 3), stk(states_s, 3)]
    return tuple(res)
```

```python
import functools

import jax
import jax.numpy as jnp
from jax import lax
from jax.experimental import pallas as pl
from jax.experimental.pallas import tpu as pltpu

F32 = jnp.float32
BF16 = jnp.bfloat16

D_MODEL = 1024
GROUP_W = 256
HEAD_DIM = 64
N_HEADS = GROUP_W // HEAD_DIM
CHUNK = 64
EPS = 1e-6
GLA_DK = 32
GLA_RANK = 16
GLA_TAU = 16.0
RG_C = 8.0
RG_CONV = 4
SC_CONV = 3
D_FF = 4 * D_MODEL
N_MOD = 6 * D_MODEL

LANE = 128
SUBLANE = 8
VMEM_LIMIT_BYTES = 56 * 1024 * 1024
MIXER_ROWS = 512
MLP_ROWS = 1024
KEY_BLOCK = 256
MLP_FF_CHUNKS = 2

SEG = {
    "qa": (0, 128), "ka": (128, 256), "va": (256, 512), "lra": (512, 640), "ra": (640, 896),
    "gb": (896, 1152), "xb": (1152, 1408), "qc": (1408, 1664),
    "bd": (1664, 1920), "cd": (1920, 2176), "xd": (2176, 2432),
}
N_IN_PAD = 2432
LRA_END = 2 * N_HEADS * GLA_DK + N_HEADS * HEAD_DIM + GLA_RANK
SBK_START = LRA_END + 4 * GROUP_W
SBV_START = SBK_START + GROUP_W
SBV_END = SBV_START + GROUP_W

LOG2E = 1.4426950408889634
SB_EXIT = -110.0 * LOG2E
SB_MASKED = -1.0e9


def _dot(a, b):
    return jnp.dot(a, b, preferred_element_type=F32)


def _dot_nt(a, b):
    return lax.dot_general(a, b, (((1,), (1,)), ((), ())), preferred_element_type=F32)


def _dot_tn(a, b):
    return lax.dot_general(a, b, (((0,), (0,)), ((), ())), preferred_element_type=F32)


def _rms(x):
    return x * lax.rsqrt(jnp.mean(x * x, axis=-1, keepdims=True) + EPS)


def _sigmoid(x):
    return 1.0 / (1.0 + jnp.exp(-x))


def _softplus(x):
    return jnp.maximum(x, 0.0) + jnp.log(1.0 + jnp.exp(-jnp.abs(x)))


def _gelu_tanh(x):
    c = 0.7978845608028654
    return x * (0.5 * (1.0 + jnp.tanh(c * (x + 0.044715 * (x * x * x)))))


def _split2(x):
    hi = x.astype(BF16)
    lo = (x - hi.astype(F32)).astype(BF16)
    return hi, lo


def _shift_rows(x, prev8, s):
    conc = jnp.concatenate([prev8, x], axis=0)
    return pltpu.roll(conc, s, 0)[SUBLANE:, :]


def _linear_scan(a, u, h0):
    rows, cols = a.shape
    a = a.reshape(rows // SUBLANE, SUBLANE, cols)
    u = u.reshape(rows // SUBLANE, SUBLANE, cols)
    row8 = lax.broadcasted_iota(jnp.int32, a.shape, 1)
    s = 1
    while s < SUBLANE:
        keep = row8 >= s
        a_sh = jnp.where(keep, pltpu.roll(a, s, 1), 1.0)
        u_sh = jnp.where(keep, pltpu.roll(u, s, 1), 0.0)
        u = a * u_sh + u
        a = a * a_sh
        s *= 2
    a = a.reshape(rows, cols)
    u = u.reshape(rows, cols)
    h = h0
    out = []
    for g in range(rows // SUBLANE):
        sl = slice(g * SUBLANE, (g + 1) * SUBLANE)
        hs = u[sl] + a[sl] * h
        out.append(hs)
        h = hs[SUBLANE - 1:SUBLANE]
    return jnp.concatenate(out, axis=0), h


def _mod_kernel(c_ref, w_ref, b_ref, o_ref):
    c = c_ref[...]
    s = (c * _sigmoid(c)).astype(BF16)
    o_ref[0] = _dot(s, w_ref[0].astype(BF16)) + b_ref[0]


def _modulation(c_all, w_ada, b_ada):
    depth = w_ada.shape[0]
    rows = c_all.shape[0]
    tn = 1024
    return pl.pallas_call(
        _mod_kernel,
        grid=(depth, N_MOD // tn),
        in_specs=[
            pl.BlockSpec((rows, D_MODEL), lambda l, j: (0, 0)),
            pl.BlockSpec((1, D_MODEL, tn), lambda l, j: (l, 0, j)),
            pl.BlockSpec((1, 1, tn), lambda l, j: (l, 0, j)),
        ],
        out_specs=pl.BlockSpec((1, rows, tn), lambda l, j: (l, 0, j)),
        out_shape=jax.ShapeDtypeStruct((depth, rows, N_MOD), F32),
        compiler_params=pltpu.CompilerParams(
            dimension_semantics=("arbitrary", "arbitrary"), vmem_limit_bytes=VMEM_LIMIT_BYTES),
        name="adaln_modulation",
    )(c_all, w_ada, b_ada.reshape(depth, 1, N_MOD))


def _mixer_kernel(*refs, tm, ta, tk, tkd, n_tiles, p_len, n_done):
    has_past = p_len > 0
    own_scratch = n_tiles > 1
    n_sub = tm // ta
    it = iter(refs)
    x_ref, mod_ref, ln1_ref, win_ref, wkt_ref, wvt_ref, wa2_ref, gba_ref = (next(it) for _ in range(8))
    rcw_ref, rcb_ref, rwa_ref, rba_ref, rwx_ref, rbx_ref, rlam_ref = (next(it) for _ in range(7))
    scw_ref, ong_ref, wout_ref, hnm_ref, cs_ref, trid_ref, trik_ref = (next(it) for _ in range(7))
    st0_ref, h0_ref, rg0_ref, sc0_ref = (next(it) for _ in range(4))
    if has_past:
        pk_ref, pv_ref = next(it), next(it)
    if n_done:
        dk_ref, dv_ref = next(it), next(it)
    x1_ref, kct_ref, vct_ref, sto_ref, ho_ref, rgo_ref, sco_ref = (next(it) for _ in range(7))
    hn_scr, st_scr, h_scr, rgp_scr, scp_scr, qm_scr, acc_scr, car_scr = (next(it) for _ in range(8))
    if own_scratch:
        kt_scr, vt_scr = next(it), next(it)

    i = pl.program_id(1)

    @pl.when(i == 0)
    def _init():
        st_scr[...] = st0_ref[0]
        h_scr[...] = h0_ref[0]
        rgp_scr[...] = rg0_ref[0]
        scp_scr[...] = sc0_ref[0]

    x = x_ref[0]
    mod = mod_ref[0]
    sh1 = mod[:, 0:D_MODEL]
    sc1 = mod[:, D_MODEL:2 * D_MODEL]
    g1 = mod[:, 2 * D_MODEL:3 * D_MODEL]
    hn = ((_rms(x) * ln1_ref[...]) * (1.0 + sc1) + sh1).astype(BF16)
    if tkd > tm:
        hn = jnp.concatenate([hn, jnp.zeros((tkd - tm, D_MODEL), BF16)], axis=0)
    hn_scr[...] = hn

    def proj(first, last=None):
        a, b = SEG[first][0], SEG[last or first][1]
        return _dot(hn_scr[0:tm, :], win_ref[:, a:b])

    ong = ong_ref[...]

    def head_norm(v, group):
        ms = _dot((v * v).astype(BF16), hnm_ref[...])
        return v * (lax.rsqrt(ms + EPS) * ong[:, group * GROUP_W:(group + 1) * GROUP_W])


    xb = proj("xb")
    rgp = rgp_scr[...]
    rcw = rcw_ref[...]
    xcv = _shift_rows(xb, rgp, 3) * rcw[0:1]
    xcv = xcv + _shift_rows(xb, rgp, 2) * rcw[1:2]
    xcv = xcv + _shift_rows(xb, rgp, 1) * rcw[2:3]
    xcv = xcv + xb * rcw[3:4]
    xcv = xcv + rcb_ref[...]
    rgp_scr[...] = xb[tm - SUBLANE:tm]
    q = proj("qc") * (HEAD_DIM ** -0.5 * LOG2E)
    gb = proj("gb")
    xcvb = xcv.astype(BF16)
    r_gate = _sigmoid(_dot(xcvb, rwa_ref[...]) + rba_ref[...])
    i_gate = _sigmoid(_dot(xcvb, rwx_ref[...]) + rbx_ref[...])
    kt = _dot_nt(wkt_ref[...], hn_scr[...])
    vt = _dot_nt(wvt_ref[...], hn_scr[...])
    log_a = (-RG_C) * r_gate * _softplus(-rlam_ref[...])
    a_sc = jnp.exp(log_a)
    u_sc = jnp.sqrt(-jnp.tanh(log_a) * (jnp.exp(2.0 * log_a) + 1.0)) * (i_gate * xcv)

    for l in range(n_done):
        kct_ref[l, 0] = dk_ref[l, 0]
        vct_ref[l, 0] = dv_ref[l, 0]
    kct_ref[n_done, 0] = kt[:, 0:tm]
    vct_ref[n_done, 0] = vt[:, 0:tm]
    lane_head = lax.broadcasted_iota(jnp.int32, (1, GROUP_W), 1) >> 6
    for s in range(n_sub):
        qs = q[s * ta:(s + 1) * ta]
        for h in range(N_HEADS):
            qm_scr[s, h * ta:(h + 1) * ta, :] = jnp.where(lane_head == h, qs, 0.0).astype(BF16)
    ktb = kt.astype(BF16)
    vtb = vt.astype(BF16)
    if own_scratch:
        for s in range(n_sub):
            kt_scr[i * n_sub + s] = ktb[:, s * tk:(s + 1) * tk]
            vt_scr[i * n_sub + s] = vtb[:, s * tk:(s + 1) * tk]
    acc_scr[...] = jnp.zeros_like(acc_scr)
    car_scr[...] = jnp.zeros_like(car_scr)

    hs, h_last = _linear_scan(a_sc, u_sc, h_scr[...])
    h_scr[...] = h_last
    o_b = head_norm(hs * _gelu_tanh(gb), 1)

    pending = {
        "qk": lambda: proj("qa", "ka"), "va": lambda: proj("va"), "lrra": lambda: proj("lra", "ra"),
        "bd": lambda: proj("bd"), "cd": lambda: proj("cd"), "xd": lambda: proj("xd"),
    }
    issued = {}

    def issue_next():
        if pending:
            name = next(iter(pending))
            issued[name] = pending.pop(name)()

    def projected(name):
        if name not in issued:
            issued[name] = pending.pop(name)()
        return issued[name]

    def attn_block(s, ktblk, vtblk, tri_ref, causal_mask, valid=None, between=lambda: None):
        tkb = ktblk.shape[1]
        lnegs, lposs = [], []
        for h in range(N_HEADS):
            z = _dot(qm_scr[s, h * ta:(h + 1) * ta, :], ktblk)
            if causal_mask:
                qrow = lax.broadcasted_iota(jnp.int32, (ta, tkb), 0)
                kcol = lax.broadcasted_iota(jnp.int32, (ta, tkb), 1)
                z = jnp.where(kcol < qrow, z, SB_MASKED)
            nz = -z
            l1p = jnp.log2(1.0 + jnp.exp2(jnp.minimum(z, nz)))
            lneg = jnp.minimum(nz, 0.0) - l1p
            lposs.append(lneg + z)
            lnegs.append(lneg.astype(BF16))
            between()
        rt = _dot(jnp.concatenate(lnegs, axis=0), tri_ref[...])
        car = car_scr[s]
        car_scr[s] = car + rt[:, tkb:tkb + LANE]
        vrow_head = lax.broadcasted_iota(jnp.int32, (GROUP_W, tkb), 0) >> 6
        ws, vs = [], []
        for h in range(N_HEADS):
            rows = slice(h * ta, (h + 1) * ta)
            carh = car[rows]
            carb = carh if tkb == LANE else jnp.concatenate([carh] * (tkb // LANE), axis=1)
            ws.append(jnp.exp2(lposs[h] + (rt[rows, 0:tkb] + carb)).astype(BF16))
            v_h = jnp.where(vrow_head == h, vtblk, jnp.zeros_like(vtblk))
            if valid is not None:
                v_h = jnp.where(valid, v_h, jnp.zeros_like(v_h))
            vs.append(v_h)
            between()
        acc_scr[s * ta:(s + 1) * ta, :] += _dot_nt(jnp.concatenate(ws, axis=1),
                                                  jnp.concatenate(vs, axis=1))

    def older_block(j):
        if has_past:
            c0 = pl.multiple_of(j * tk, tk)
            return (pk_ref[0, 0, :, pl.ds(c0, tk)].astype(BF16),
                    pv_ref[0, 0, :, pl.ds(c0, tk)].astype(BF16))
        return kt_scr[j], vt_scr[j]

    for s in range(n_sub):
        attn_block(s, ktb[:, s * tkd:(s + 1) * tkd], vtb[:, s * tkd:(s + 1) * tkd], trid_ref, True,
                   between=issue_next)
        if has_past or own_scratch:
            if has_past:
                kp, vp = older_block(p_len // tk - 1)
                attn_block(s, kp, vp, trik_ref, False, between=issue_next)
            elif s > 0:
                attn_block(s, ktb[:, (s - 1) * tk:s * tk], vtb[:, (s - 1) * tk:s * tk], trik_ref,
                           False, between=issue_next)
            else:
                kp, vp = older_block(jnp.maximum(i * n_sub - 1, 0))
                attn_block(s, kp, vp, trik_ref, False, valid=i >= 1, between=issue_next)

    qk = projected("qk")
    va = projected("va")
    lrra = projected("lrra")
    lra = lrra[:, 0:LANE]
    ra = lrra[:, LANE:LANE + GROUP_W]
    qa = qk[:, 0:LANE] * (GLA_DK ** -0.5)
    ka = qk[:, LANE:2 * LANE]
    la = -_softplus(-(_dot(lra.astype(BF16), wa2_ref[...]) + gba_ref[...])) * (1.0 / GLA_TAU)
    la_hi, la_lo = _split2(la)
    gg = _dot(cs_ref[...], jnp.concatenate([la_hi, la_lo], axis=1))
    gg = gg[:, 0:LANE] + gg[:, LANE:2 * LANE]
    g_cum = gg[0:tm]
    g_end = gg[tm:2 * tm]
    kd = (ka * jnp.exp(g_end - g_cum)).astype(BF16)
    dec = jnp.exp(g_end)
    vab = va.astype(BF16)
    qab = qa.astype(BF16)
    st_row = lax.broadcasted_iota(jnp.int32, (GROUP_W, N_HEADS * GLA_DK), 0) >> 6
    st_col = lax.broadcasted_iota(jnp.int32, (GROUP_W, N_HEADS * GLA_DK), 1) >> 5
    st_mask = st_row == st_col
    n_chunks = tm // CHUNK
    uts = [_dot_tn(vab[c * CHUNK:(c + 1) * CHUNK], kd[c * CHUNK:(c + 1) * CHUNK])
           for c in range(n_chunks)]
    st = st_scr[...]
    sts = []
    for c in range(n_chunks):
        st = st * dec[c * CHUNK:c * CHUNK + 1] + jnp.where(st_mask, uts[c], 0.0)
        sts.append(st.astype(BF16))
    st_scr[...] = st
    o_chunks = [_dot_nt(qab[c * CHUNK:(c + 1) * CHUNK], sts[c]) for c in range(n_chunks)]
    o_gla = o_chunks[0] if n_chunks == 1 else jnp.concatenate(o_chunks, axis=0)
    o_a = head_norm(o_gla, 0) * (ra * _sigmoid(ra))

    bd = projected("bd")
    cx = projected("cd") * projected("xd")
    scp = scp_scr[...]
    scw = scw_ref[...]
    yd = _shift_rows(cx, scp, 2) * scw[0:1]
    yd = yd + _shift_rows(cx, scp, 1) * scw[1:2]
    yd = yd + cx * scw[2:3]
    scp_scr[...] = cx[tm - SUBLANE:tm]
    o_d = head_norm(bd * yd, 3)

    if has_past or own_scratch:
        for s in range(n_sub):
            def cond(c):
                j, m = c
                return jnp.logical_and(j >= 0, m > SB_EXIT)

            def body(c, s=s):
                j, _ = c
                kblk, vblk = older_block(j)
                attn_block(s, kblk, vblk, trik_ref, False)
                return j - 1, jnp.max(car_scr[s])

            first = (p_len // tk - 2) if has_past else (i * n_sub + s - 2)
            lax.while_loop(cond, body, (first, jnp.max(car_scr[s])))
    o_c = head_norm(acc_scr[...], 2)

    mix = _dot(o_a.astype(BF16), wout_ref[0:GROUP_W, :])
    mix = mix + _dot(o_b.astype(BF16), wout_ref[GROUP_W:2 * GROUP_W, :])
    mix = mix + _dot(o_c.astype(BF16), wout_ref[2 * GROUP_W:3 * GROUP_W, :])
    mix = mix + _dot(o_d.astype(BF16), wout_ref[3 * GROUP_W:4 * GROUP_W, :])
    x1_ref[0] = x + g1 * mix

    @pl.when(i == n_tiles - 1)
    def _fin():
        sto_ref[0] = st_scr[...]
        ho_ref[0] = h_scr[...]
        rgo_ref[0] = rgp_scr[...]
        sco_ref[0] = scp_scr[...]


def _mixer_tiles(t, p_len):
    tm = min(t, MIXER_ROWS)
    tk = KEY_BLOCK
    ta = min(tm, tk)
    tkd = max(ta, LANE)
    assert t % tm == 0 and tm % CHUNK == 0 and tm % SUBLANE == 0 and tm % ta == 0
    n_tiles = t // tm
    assert p_len == 0 or n_tiles == 1, "cached keys are only supported with a single row tile"
    assert p_len % tk == 0 and (n_tiles == 1 or ta == tk)
    return tm, ta, tk, tkd, n_tiles


def _tri_ones(n):
    s = jnp.arange(n)[:, None]
    j = jnp.arange(n)[None, :]
    return jnp.concatenate([(s > j).astype(BF16), jnp.ones((n, LANE), BF16)], axis=1)


def _cumsum_mats(tm):
    t = jnp.arange(tm)[:, None]
    s = jnp.arange(tm)[None, :]
    same = (t // CHUNK) == (s // CHUNK)
    return jnp.concatenate([(same & (s <= t)).astype(BF16), same.astype(BF16)], axis=0)


def _const_spec(shape):
    nd = len(shape)
    return pl.BlockSpec(shape, lambda b, i: (0,) * nd)


def _mixer(x, mod, w, state, past, layer, done=None):
    bsz, t, _ = x.shape
    p_len = 0 if past is None else past[0].shape[3]
    n_done = 0 if done is None else done[0].shape[0]
    tm, ta, tk, tkd, n_tiles = _mixer_tiles(t, p_len)
    tri_d = _tri_ones(tkd)
    tri_k = _tri_ones(tk)
    cs = _cumsum_mats(tm)
    per_b = lambda b, i: (b, 0, 0)
    tile = lambda b, i: (b, i, 0)
    tile_t = lambda b, i: (0, b, 0, i)

    consts = [w["ln1_g"], w["w_in"], w["wkt"], w["wvt"], w["wa2"], w["gla_ba"], w["rg_conv_w"],
              w["rg_conv_b"], w["rg_wa"], w["rg_ba"], w["rg_wx"], w["rg_bx"], w["rg_lambda"],
              w["sc_conv_w"], w["out_norm_g"], w["w_out"], w["hnm"], cs, tri_d, tri_k]
    in_specs = [pl.BlockSpec((1, tm, D_MODEL), tile), pl.BlockSpec((1, 1, N_MOD), per_b)]
    in_specs += [_const_spec(c.shape) for c in consts]
    in_specs += [pl.BlockSpec((1,) + s.shape[1:], per_b) for s in state]
    args = [x, mod] + consts + list(state)
    if past is not None:
        in_specs += [pl.BlockSpec((1, 1, GROUP_W, p_len), lambda b, i: (layer, b, 0, 0))] * 2
        args += list(past)
    if n_done:
        in_specs += [pl.BlockSpec((n_done, 1, GROUP_W, tm), tile_t)] * 2
        args += list(done)

    out_shape = [
        jax.ShapeDtypeStruct((bsz, t, D_MODEL), F32),
        jax.ShapeDtypeStruct((n_done + 1, bsz, GROUP_W, t), F32),
        jax.ShapeDtypeStruct((n_done + 1, bsz, GROUP_W, t), F32),
    ] + [jax.ShapeDtypeStruct(s.shape, F32) for s in state]
    out_specs = [pl.BlockSpec((1, tm, D_MODEL), tile),
                 pl.BlockSpec((n_done + 1, 1, GROUP_W, tm), tile_t),
                 pl.BlockSpec((n_done + 1, 1, GROUP_W, tm), tile_t)]
    out_specs += [pl.BlockSpec((1,) + s.shape[1:], per_b) for s in state]

    scratch = [
        pltpu.VMEM((max(tm, tkd), D_MODEL), BF16),
        pltpu.VMEM((GROUP_W, N_HEADS * GLA_DK), F32),
        pltpu.VMEM((1, GROUP_W), F32),
        pltpu.VMEM((SUBLANE, GROUP_W), F32),
        pltpu.VMEM((SUBLANE, GROUP_W), F32),
        pltpu.VMEM((tm // ta, N_HEADS * ta, GROUP_W), BF16),
        pltpu.VMEM((tm, GROUP_W), F32),
        pltpu.VMEM((tm // ta, N_HEADS * ta, LANE), F32),
    ]
    if n_tiles > 1:
        scratch += [pltpu.VMEM((t // tk, GROUP_W, tk), BF16)] * 2

    kern = functools.partial(_mixer_kernel, tm=tm, ta=ta, tk=tk, tkd=tkd, n_tiles=n_tiles,
                             p_len=p_len, n_done=n_done)
    return pl.pallas_call(
        kern,
        grid=(bsz, n_tiles),
        in_specs=in_specs,
        out_specs=out_specs,
        out_shape=out_shape,
        scratch_shapes=scratch,
        compiler_params=pltpu.CompilerParams(
            dimension_semantics=("arbitrary", "arbitrary"), vmem_limit_bytes=VMEM_LIMIT_BYTES),
        name="mixer_t%d_p%d_l%d" % (t, p_len, n_done),
    )(*args)


def _mlp_kernel(x_ref, mod_ref, g_ref, w1_ref, w2_ref, fg_ref, o_ref, *, final):
    x = x_ref[...]
    bb, tm, _ = x.shape
    mod = mod_ref[...]
    sh2 = mod[:, :, 3 * D_MODEL:4 * D_MODEL]
    sc2 = mod[:, :, 4 * D_MODEL:5 * D_MODEL]
    g2 = mod[:, :, 5 * D_MODEL:6 * D_MODEL]
    h2 = ((_rms(x) * g_ref[...]) * (1.0 + sc2) + sh2).astype(BF16).reshape(bb * tm, D_MODEL)
    fc = D_FF // MLP_FF_CHUNKS
    acc = None
    for c in range(MLP_FF_CHUNKS):
        u = jnp.maximum(_dot(h2, w1_ref[:, c * fc:(c + 1) * fc]), 0.0)
        part = _dot((u * u).astype(BF16), w2_ref[c * fc:(c + 1) * fc, :])
        acc = part if acc is None else acc + part
    y = x + g2 * acc.reshape(bb, tm, D_MODEL)
    if final:
        y = _rms(y) * fg_ref[...]
    o_ref[...] = y


def _mlp(x, mod, ln2_g, w1, w2, final_g, final):
    bsz, t, _ = x.shape
    tm = min(t, MLP_ROWS)
    bb = 1 if tm == MLP_ROWS else min(bsz, MLP_ROWS // tm)
    assert t % tm == 0 and bsz % bb == 0
    tile = lambda b, i: (b, i, 0)
    return pl.pallas_call(
        functools.partial(_mlp_kernel, final=final),
        grid=(bsz // bb, t // tm),
        in_specs=[
            pl.BlockSpec((bb, tm, D_MODEL), tile),
            pl.BlockSpec((bb, 1, N_MOD), lambda b, i: (b, 0, 0)),
            _const_spec(ln2_g.shape),
            pl.BlockSpec(w1.shape, lambda b, i: (0, 0), pipeline_mode=pl.Buffered(1)),
            pl.BlockSpec(w2.shape, lambda b, i: (0, 0), pipeline_mode=pl.Buffered(1)),
            _const_spec(final_g.shape),
        ],
        out_specs=pl.BlockSpec((bb, tm, D_MODEL), tile),
        out_shape=jax.ShapeDtypeStruct(x.shape, F32),
        compiler_params=pltpu.CompilerParams(
            dimension_semantics=("parallel", "parallel"), vmem_limit_bytes=VMEM_LIMIT_BYTES),
        name="mlp_t%d%s" % (t, "_final" if final else ""),
    )(x, mod, ln2_g, w1, w2, final_g)


def _block_diag(blocks):
    n, a, b = blocks.shape
    eye = jnp.eye(n, dtype=blocks.dtype)
    return jnp.einsum("nij,nm->nimj", blocks, eye).reshape(n * a, n * b)


def _layer_weights(l, ln1_g, w_in, gla_wa2, gla_ba, rg_conv_w, rg_conv_b, rg_wa, rg_ba, rg_wx,
                   rg_bx, rg_lambda, sc_conv_w, out_norm_g, w_out):
    wi = w_in[l]
    w_in_p = jnp.concatenate(
        [wi[:, :LRA_END], jnp.zeros((D_MODEL, LANE - GLA_RANK), wi.dtype), wi[:, LRA_END:SBK_START],
         wi[:, SBV_END:]], axis=1)
    wa2 = jnp.zeros((LANE, N_HEADS * GLA_DK), F32).at[:GLA_RANK].set(gla_wa2[l])
    hnm = _block_diag(jnp.full((N_HEADS, HEAD_DIM, HEAD_DIM), 1.0 / HEAD_DIM, F32))
    row = lambda v: v[l].reshape(1, -1)
    return {
        "ln1_g": row(ln1_g), "w_in": w_in_p.astype(BF16),
        "wkt": wi[:, SBK_START:SBV_START].T.astype(BF16), "wvt": wi[:, SBV_START:SBV_END].T.astype(BF16),
        "wa2": wa2.astype(BF16),
        "gla_ba": row(gla_ba), "rg_conv_w": rg_conv_w[l], "rg_conv_b": row(rg_conv_b),
        "rg_wa": _block_diag(rg_wa[l]).astype(BF16), "rg_ba": row(rg_ba),
        "rg_wx": _block_diag(rg_wx[l]).astype(BF16), "rg_bx": row(rg_bx),
        "rg_lambda": row(rg_lambda), "sc_conv_w": sc_conv_w[l], "out_norm_g": row(out_norm_g),
        "w_out": w_out[l].astype(BF16), "hnm": hnm.astype(BF16),
    }


def _state_in(s_gla, h_rg, buf_rg, buf_sc):
    bsz = s_gla.shape[0]
    eye = jnp.eye(N_HEADS, dtype=F32)
    st = jnp.einsum("bhkv,hg->bhvgk", s_gla.astype(F32), eye).reshape(bsz, GROUP_W, N_HEADS * GLA_DK)
    pad8 = lambda buf: jnp.concatenate(
        [jnp.zeros((bsz, SUBLANE - buf.shape[1], GROUP_W), F32), buf.astype(F32)], axis=1)
    return st, h_rg.astype(F32).reshape(bsz, 1, GROUP_W), pad8(buf_rg), pad8(buf_sc)


def _state_out(st, h, rgbuf, scbuf):
    bsz = st.shape[0]
    st5 = st.reshape(bsz, N_HEADS, HEAD_DIM, N_HEADS, GLA_DK)
    s_gla = jnp.stack([st5[:, h_, :, h_, :] for h_ in range(N_HEADS)], axis=1)
    return (jnp.swapaxes(s_gla, 2, 3), h.reshape(bsz, GROUP_W),
            rgbuf[:, SUBLANE - (RG_CONV - 1):], scbuf[:, SUBLANE - (SC_CONV - 1):])


def _rows_from_transposed(xt):
    l, b, _, t = xt.shape
    return jnp.transpose(xt.reshape(l, b, N_HEADS, HEAD_DIM, t), (0, 1, 4, 2, 3))


def kernel(x_prompt, x_sample, c_prompt, c_sample, state_gla, state_rg_h, state_rg_conv, cache_sb_k, cache_sb_v, state_sc_conv, ln1_g, ln2_g, w_ada, b_ada, w_in, gla_wa2, gla_ba, rg_conv_w, rg_conv_b, rg_wa, rg_ba, rg_wx, rg_bx, rg_lambda, sc_conv_w, out_norm_g, w_out, mlp_w1, mlp_w2, final_g):
    depth = w_in.shape[0]
    bp, bs = x_prompt.shape[0], x_sample.shape[0]
    p_len = cache_sb_k.shape[2]

    rows = -(-(bp + bs) // 16) * 16
    c_all = jnp.concatenate(
        [c_prompt, c_sample, jnp.zeros((rows - bp - bs, D_MODEL), F32)], axis=0)
    mod_all = _modulation(c_all, w_ada, b_ada)

    zero_state = _state_in(jnp.zeros((bp, N_HEADS, GLA_DK, HEAD_DIM), F32), jnp.zeros((bp, GROUP_W), F32),
                           jnp.zeros((bp, RG_CONV - 1, GROUP_W), F32), jnp.zeros((bp, SC_CONV - 1, GROUP_W), F32))
    fg = final_g.reshape(1, D_MODEL)
    past = tuple(jnp.transpose(c, (0, 1, 3, 4, 2)).reshape(depth, bs, GROUP_W, p_len)
                 for c in (cache_sb_k, cache_sb_v))

    xp, xs = x_prompt, x_sample
    kv_p, kv_s = None, None
    states_p, states_s = [], []
    for l in range(depth):
        w = _layer_weights(l, ln1_g, w_in, gla_wa2, gla_ba, rg_conv_w, rg_conv_b, rg_wa, rg_ba,
                           rg_wx, rg_bx, rg_lambda, sc_conv_w, out_norm_g, w_out)
        w1 = mlp_w1[l].astype(BF16)
        w2 = mlp_w2[l].astype(BF16)
        g2 = ln2_g[l].reshape(1, D_MODEL)
        mod_p = mod_all[l, :bp].reshape(bp, 1, N_MOD)
        mod_s = mod_all[l, bp:bp + bs].reshape(bs, 1, N_MOD)
        final = l == depth - 1

        rp = _mixer(xp, mod_p, w, zero_state, None, l, kv_p)
        sample_state = _state_in(state_gla[l], state_rg_h[l], state_rg_conv[l], state_sc_conv[l])
        rs = _mixer(xs, mod_s, w, sample_state, past, l, kv_s)
        kv_p, kv_s = (rp[1], rp[2]), (rs[1], rs[2])

        xp = _mlp(rp[0], mod_p, g2, w1, w2, fg, final)
        xs = _mlp(rs[0], mod_s, g2, w1, w2, fg, final)
        states_p.append(_state_out(*rp[3:7]))
        states_s.append(_state_out(*rs[3:7]))

    def stk(lst, j):
        return jnp.stack([st[j] for st in lst], axis=0)

    res = [xp, xs]
    for j in range(3):
        res += [stk(states_p, j), stk(states_s, j)]
    res += [_rows_from_transposed(kv_p[0]), _rows_from_transposed(kv_s[0]),
            _rows_from_transposed(kv_p[1]), _rows_from_transposed(kv_s[1]),
            stk(states_p, 3), stk(states_s, 3)]
    return tuple(res)
```

```python
import functools

import jax
import jax.numpy as jnp
from jax import lax
from jax.experimental import pallas as pl
from jax.experimental.pallas import tpu as pltpu

F32 = jnp.float32
BF16 = jnp.bfloat16

D_MODEL = 1024
GROUP_W = 256
HEAD_DIM = 64
N_HEADS = GROUP_W // HEAD_DIM
CHUNK = 64
EPS = 1e-6
GLA_DK = 32
GLA_RANK = 16
GLA_TAU = 16.0
RG_C = 8.0
RG_CONV = 4
SC_CONV = 3
D_FF = 4 * D_MODEL
N_MOD = 6 * D_MODEL

LANE = 128
SUBLANE = 8
VMEM_LIMIT_BYTES = 56 * 1024 * 1024
MIXER_ROWS = 512
MLP_ROWS = 1024
KEY_BLOCK = 256
MLP_FF_CHUNKS = 2

SEG = {
    "qa": (0, 128), "ka": (128, 256), "va": (256, 512), "lra": (512, 640), "ra": (640, 896),
    "gb": (896, 1152), "xb": (1152, 1408), "qc": (1408, 1664),
    "bd": (1664, 1920), "cd": (1920, 2176), "xd": (2176, 2432),
}
N_IN_PAD = 2432
LRA_END = 2 * N_HEADS * GLA_DK + N_HEADS * HEAD_DIM + GLA_RANK
SBK_START = LRA_END + 4 * GROUP_W
SBV_START = SBK_START + GROUP_W
SBV_END = SBV_START + GROUP_W

LOG2E = 1.4426950408889634
SB_EXIT = -110.0 * LOG2E
SB_MASKED = -1.0e9


def _dot(a, b):
    return jnp.dot(a, b, preferred_element_type=F32)


def _dot_nt(a, b):
    return lax.dot_general(a, b, (((1,), (1,)), ((), ())), preferred_element_type=F32)


def _dot_tn(a, b):
    return lax.dot_general(a, b, (((0,), (0,)), ((), ())), preferred_element_type=F32)


def _rms(x):
    return x * lax.rsqrt(jnp.mean(x * x, axis=-1, keepdims=True) + EPS)


def _sigmoid(x):
    return 1.0 / (1.0 + jnp.exp(-x))


def _softplus(x):
    return jnp.maximum(x, 0.0) + jnp.log(1.0 + jnp.exp(-jnp.abs(x)))


def _gelu_tanh(x):
    c = 0.7978845608028654
    return x * (0.5 * (1.0 + jnp.tanh(c * (x + 0.044715 * (x * x * x)))))


def _split2(x):
    hi = x.astype(BF16)
    lo = (x - hi.astype(F32)).astype(BF16)
    return hi, lo


def _shift_rows(x, prev8, s):
    conc = jnp.concatenate([prev8, x], axis=0)
    return pltpu.roll(conc, s, 0)[SUBLANE:, :]


def _linear_scan(a, u, h0):
    rows, cols = a.shape
    a = a.reshape(rows // SUBLANE, SUBLANE, cols)
    u = u.reshape(rows // SUBLANE, SUBLANE, cols)
    row8 = lax.broadcasted_iota(jnp.int32, a.shape, 1)
    s = 1
    while s < SUBLANE:
        keep = row8 >= s
        a_sh = jnp.where(keep, pltpu.roll(a, s, 1), 1.0)
        u_sh = jnp.where(keep, pltpu.roll(u, s, 1), 0.0)
        u = a * u_sh + u
        a = a * a_sh
        s *= 2
    a = a.reshape(rows, cols)
    u = u.reshape(rows, cols)
    h = h0
    out = []
    for g in range(rows // SUBLANE):
        sl = slice(g * SUBLANE, (g + 1) * SUBLANE)
        hs = u[sl] + a[sl] * h
        out.append(hs)
        h = hs[SUBLANE - 1:SUBLANE]
    return jnp.concatenate(out, axis=0), h


def _mod_kernel(c_ref, w_ref, b_ref, o_ref):
    c = c_ref[...]
    s = (c * _sigmoid(c)).astype(BF16)
    o_ref[0] = _dot(s, w_ref[0].astype(BF16)) + b_ref[0]


def _modulation(c_all, w_ada, b_ada):
    depth = w_ada.shape[0]
    rows = c_all.shape[0]
    tn = 1024
    return pl.pallas_call(
        _mod_kernel,
        grid=(depth, N_MOD // tn),
        in_specs=[
            pl.BlockSpec((rows, D_MODEL), lambda l, j: (0, 0)),
            pl.BlockSpec((1, D_MODEL, tn), lambda l, j: (l, 0, j)),
            pl.BlockSpec((1, 1, tn), lambda l, j: (l, 0, j)),
        ],
        out_specs=pl.BlockSpec((1, rows, tn), lambda l, j: (l, 0, j)),
        out_shape=jax.ShapeDtypeStruct((depth, rows, N_MOD), F32),
        compiler_params=pltpu.CompilerParams(
            dimension_semantics=("arbitrary", "arbitrary"), vmem_limit_bytes=VMEM_LIMIT_BYTES),
        name="adaln_modulation",
    )(c_all, w_ada, b_ada.reshape(depth, 1, N_MOD))


def _mixer_kernel(*refs, tm, ta, tk, tkd, n_tiles, p_len, n_done):
    has_past = p_len > 0
    own_scratch = n_tiles > 1
    n_sub = tm // ta
    it = iter(refs)
    x_ref, mod_ref, ln1_ref, win_ref, wkt_ref, wvt_ref, wa2_ref, gba_ref = (next(it) for _ in range(8))
    rcw_ref, rcb_ref, rwa_ref, rba_ref, rwx_ref, rbx_ref, rlam_ref = (next(it) for _ in range(7))
    scw_ref, ong_ref, wout_ref, hnm_ref, cs_ref, trid_ref, trik_ref = (next(it) for _ in range(7))
    st0_ref, h0_ref, rg0_ref, sc0_ref = (next(it) for _ in range(4))
    if has_past:
        pk_ref, pv_ref = next(it), next(it)
    if n_done:
        dk_ref, dv_ref = next(it), next(it)
    x1_ref, kct_ref, vct_ref, sto_ref, ho_ref, rgo_ref, sco_ref = (next(it) for _ in range(7))
    hn_scr, st_scr, h_scr, rgp_scr, scp_scr, qm_scr, acc_scr, car_scr = (next(it) for _ in range(8))
    if own_scratch:
        kt_scr, vt_scr = next(it), next(it)

    i = pl.program_id(1)

    @pl.when(i == 0)
    def _init():
        st_scr[...] = st0_ref[0]
        h_scr[...] = h0_ref[0]
        rgp_scr[...] = rg0_ref[0]
        scp_scr[...] = sc0_ref[0]

    x = x_ref[0]
    mod = mod_ref[0]
    sh1 = mod[:, 0:D_MODEL]
    sc1 = mod[:, D_MODEL:2 * D_MODEL]
    g1 = mod[:, 2 * D_MODEL:3 * D_MODEL]
    hn = ((_rms(x) * ln1_ref[...]) * (1.0 + sc1) + sh1).astype(BF16)
    if tkd > tm:
        hn = jnp.concatenate([hn, jnp.zeros((tkd - tm, D_MODEL), BF16)], axis=0)
    hn_scr[...] = hn

    def proj(first, last=None):
        a, b = SEG[first][0], SEG[last or first][1]
        return _dot(hn_scr[0:tm, :], win_ref[:, a:b])

    ong = ong_ref[...]

    def head_norm(v, group):
        ms = _dot((v * v).astype(BF16), hnm_ref[...])
        return v * (lax.rsqrt(ms + EPS) * ong[:, group * GROUP_W:(group + 1) * GROUP_W])


    xb = proj("xb")
    rgp = rgp_scr[...]
    rcw = rcw_ref[...]
    xcv = _shift_rows(xb, rgp, 3) * rcw[0:1]
    xcv = xcv + _shift_rows(xb, rgp, 2) * rcw[1:2]
    xcv = xcv + _shift_rows(xb, rgp, 1) * rcw[2:3]
    xcv = xcv + xb * rcw[3:4]
    xcv = xcv + rcb_ref[...]
    rgp_scr[...] = xb[tm - SUBLANE:tm]
    q = proj("qc") * (HEAD_DIM ** -0.5 * LOG2E)
    gb = proj("gb")
    xcvb = xcv.astype(BF16)
    r_gate = _sigmoid(_dot(xcvb, rwa_ref[...]) + rba_ref[...])
    i_gate = _sigmoid(_dot(xcvb, rwx_ref[...]) + rbx_ref[...])
    kt = _dot_nt(wkt_ref[...], hn_scr[...])
    vt = _dot_nt(wvt_ref[...], hn_scr[...])
    log_a = (-RG_C) * r_gate * _softplus(-rlam_ref[...])
    a_sc = jnp.exp(log_a)
    u_sc = jnp.sqrt(-jnp.tanh(log_a) * (jnp.exp(2.0 * log_a) + 1.0)) * (i_gate * xcv)

    for l in range(n_done):
        kct_ref[l, 0] = dk_ref[l, 0]
        vct_ref[l, 0] = dv_ref[l, 0]
    kct_ref[n_done, 0] = kt[:, 0:tm]
    vct_ref[n_done, 0] = vt[:, 0:tm]
    lane_head = lax.broadcasted_iota(jnp.int32, (1, GROUP_W), 1) >> 6
    for s in range(n_sub):
        qs = q[s * ta:(s + 1) * ta]
        for h in range(N_HEADS):
            qm_scr[s, h * ta:(h + 1) * ta, :] = jnp.where(lane_head == h, qs, 0.0).astype(BF16)
    ktb = kt.astype(BF16)
    vtb = vt.astype(BF16)
    if own_scratch:
        for s in range(n_sub):
            kt_scr[i * n_sub + s] = ktb[:, s * tk:(s + 1) * tk]
            vt_scr[i * n_sub + s] = vtb[:, s * tk:(s + 1) * tk]
    acc_scr[...] = jnp.zeros_like(acc_scr)
    car_scr[...] = jnp.zeros_like(car_scr)

    hs, h_last = _linear_scan(a_sc, u_sc, h_scr[...])
    h_scr[...] = h_last
    o_b = head_norm(hs * _gelu_tanh(gb), 1)

    pending = {
        "qk": lambda: proj("qa", "ka"), "va": lambda: proj("va"), "lrra": lambda: proj("lra", "ra"),
        "bd": lambda: proj("bd"), "cd": lambda: proj("cd"), "xd": lambda: proj("xd"),
    }
    issued = {}

    def issue_next():
        if pending:
            name = next(iter(pending))
            issued[name] = pending.pop(name)()

    def projected(name):
        if name not in issued:
            issued[name] = pending.pop(name)()
        return issued[name]

    def attn_block(s, ktblk, vtblk, tri_ref, causal_mask, valid=None, between=lambda: None):
        tkb = ktblk.shape[1]
        lnegs, lposs = [], []
        for h in range(N_HEADS):
            z = _dot(qm_scr[s, h * ta:(h + 1) * ta, :], ktblk)
            if causal_mask:
                qrow = lax.broadcasted_iota(jnp.int32, (ta, tkb), 0)
                kcol = lax.broadcasted_iota(jnp.int32, (ta, tkb), 1)
                z = jnp.where(kcol < qrow, z, SB_MASKED)
            nz = -z
            l1p = jnp.log2(1.0 + jnp.exp2(jnp.minimum(z, nz)))
            lneg = jnp.minimum(nz, 0.0) - l1p
            lposs.append(lneg + z)
            lnegs.append(lneg.astype(BF16))
            between()
        rt = _dot(jnp.concatenate(lnegs, axis=0), tri_ref[...])
        car = car_scr[s]
        car_scr[s] = car + rt[:, tkb:tkb + LANE]
        vrow_head = lax.broadcasted_iota(jnp.int32, (GROUP_W, tkb), 0) >> 6
        ws, vs = [], []
        for h in range(N_HEADS):
            rows = slice(h * ta, (h + 1) * ta)
            carh = car[rows]
            carb = carh if tkb == LANE else jnp.concatenate([carh] * (tkb // LANE), axis=1)
            ws.append(jnp.exp2(lposs[h] + (rt[rows, 0:tkb] + carb)).astype(BF16))
            v_h = jnp.where(vrow_head == h, vtblk, jnp.zeros_like(vtblk))
            if valid is not None:
                v_h = jnp.where(valid, v_h, jnp.zeros_like(v_h))
            vs.append(v_h)
            between()
        acc_scr[s * ta:(s + 1) * ta, :] += _dot_nt(jnp.concatenate(ws, axis=1),
                                                  jnp.concatenate(vs, axis=1))

    def older_block(j):
        if has_past:
            c0 = pl.multiple_of(j * tk, tk)
            return (pk_ref[0, 0, :, pl.ds(c0, tk)].astype(BF16),
                    pv_ref[0, 0, :, pl.ds(c0, tk)].astype(BF16))
        return kt_scr[j], vt_scr[j]

    for s in range(n_sub):
        attn_block(s, ktb[:, s * tkd:(s + 1) * tkd], vtb[:, s * tkd:(s + 1) * tkd], trid_ref, True,
                   between=issue_next)
        if has_past or own_scratch:
            if has_past:
                kp, vp = older_block(p_len // tk - 1)
                attn_block(s, kp, vp, trik_ref, False, between=issue_next)
            elif s > 0:
                attn_block(s, ktb[:, (s - 1) * tk:s * tk], vtb[:, (s - 1) * tk:s * tk], trik_ref,
                           False, between=issue_next)
            else:
                kp, vp = older_block(jnp.maximum(i * n_sub - 1, 0))
                attn_block(s, kp, vp, trik_ref, False, valid=i >= 1, between=issue_next)

    qk = projected("qk")
    va = projected("va")
    lrra = projected("lrra")
    lra = lrra[:, 0:LANE]
    ra = lrra[:, LANE:LANE + GROUP_W]
    qa = qk[:, 0:LANE] * (GLA_DK ** -0.5)
    ka = qk[:, LANE:2 * LANE]
    la = -_softplus(-(_dot(lra.astype(BF16), wa2_ref[...]) + gba_ref[...])) * (1.0 / GLA_TAU)
    la_hi, la_lo = _split2(la)
    lac = jnp.concatenate([la_hi, la_lo], axis=1)
    tc = cs_ref.shape[1]
    g_cums, g_ends = [], []
    for r in range(tm // tc):
        gg = _dot(cs_ref[...], lac[r * tc:(r + 1) * tc])
        gg = gg[:, 0:LANE] + gg[:, LANE:2 * LANE]
        g_cums.append(gg[0:tc])
        g_ends.append(gg[tc:2 * tc])
    g_cum = g_cums[0] if len(g_cums) == 1 else jnp.concatenate(g_cums, axis=0)
    g_end = g_ends[0] if len(g_ends) == 1 else jnp.concatenate(g_ends, axis=0)
    kd = (ka * jnp.exp(g_end - g_cum)).astype(BF16)
    dec = jnp.exp(g_end)
    vab = va.astype(BF16)
    qab = qa.astype(BF16)
    st_row = lax.broadcasted_iota(jnp.int32, (GROUP_W, N_HEADS * GLA_DK), 0) >> 6
    st_col = lax.broadcasted_iota(jnp.int32, (GROUP_W, N_HEADS * GLA_DK), 1) >> 5
    st_mask = st_row == st_col
    n_chunks = tm // CHUNK
    uts = [_dot_tn(vab[c * CHUNK:(c + 1) * CHUNK], kd[c * CHUNK:(c + 1) * CHUNK])
           for c in range(n_chunks)]
    st = st_scr[...]
    sts = []
    for c in range(n_chunks):
        st = st * dec[c * CHUNK:c * CHUNK + 1] + jnp.where(st_mask, uts[c], 0.0)
        sts.append(st.astype(BF16))
    st_scr[...] = st
    o_chunks = [_dot_nt(qab[c * CHUNK:(c + 1) * CHUNK], sts[c]) for c in range(n_chunks)]
    o_gla = o_chunks[0] if n_chunks == 1 else jnp.concatenate(o_chunks, axis=0)
    o_a = head_norm(o_gla, 0) * (ra * _sigmoid(ra))

    bd = projected("bd")
    cx = projected("cd") * projected("xd")
    scp = scp_scr[...]
    scw = scw_ref[...]
    yd = _shift_rows(cx, scp, 2) * scw[0:1]
    yd = yd + _shift_rows(cx, scp, 1) * scw[1:2]
    yd = yd + cx * scw[2:3]
    scp_scr[...] = cx[tm - SUBLANE:tm]
    o_d = head_norm(bd * yd, 3)

    if has_past or own_scratch:
        for s in range(n_sub):
            def cond(c):
                j, m = c
                return jnp.logical_and(j >= 0, m > SB_EXIT)

            def body(c, s=s):
                j, _ = c
                kblk, vblk = older_block(j)
                attn_block(s, kblk, vblk, trik_ref, False)
                return j - 1, jnp.max(car_scr[s])

            first = (p_len // tk - 2) if has_past else (i * n_sub + s - 2)
            lax.while_loop(cond, body, (first, jnp.max(car_scr[s])))
    o_c = head_norm(acc_scr[...], 2)

    mix = _dot(o_a.astype(BF16), wout_ref[0:GROUP_W, :])
    mix = mix + _dot(o_b.astype(BF16), wout_ref[GROUP_W:2 * GROUP_W, :])
    mix = mix + _dot(o_c.astype(BF16), wout_ref[2 * GROUP_W:3 * GROUP_W, :])
    mix = mix + _dot(o_d.astype(BF16), wout_ref[3 * GROUP_W:4 * GROUP_W, :])
    x1_ref[0] = x + g1 * mix

    @pl.when(i == n_tiles - 1)
    def _fin():
        sto_ref[0] = st_scr[...]
        ho_ref[0] = h_scr[...]
        rgo_ref[0] = rgp_scr[...]
        sco_ref[0] = scp_scr[...]


def _mixer_tiles(t, p_len):
    tm = min(t, MIXER_ROWS)
    tk = KEY_BLOCK
    ta = min(tm, tk)
    tkd = max(ta, LANE)
    assert t % tm == 0 and tm % CHUNK == 0 and tm % SUBLANE == 0 and tm % ta == 0
    n_tiles = t // tm
    assert p_len == 0 or n_tiles == 1, "cached keys are only supported with a single row tile"
    assert p_len % tk == 0 and (n_tiles == 1 or ta == tk)
    return tm, ta, tk, tkd, n_tiles


def _tri_ones(n):
    s = jnp.arange(n)[:, None]
    j = jnp.arange(n)[None, :]
    return jnp.concatenate([(s > j).astype(BF16), jnp.ones((n, LANE), BF16)], axis=1)


def _cumsum_mats(tm):
    t = jnp.arange(tm)[:, None]
    s = jnp.arange(tm)[None, :]
    same = (t // CHUNK) == (s // CHUNK)
    return jnp.concatenate([(same & (s <= t)).astype(BF16), same.astype(BF16)], axis=0)


def _const_spec(shape):
    nd = len(shape)
    return pl.BlockSpec(shape, lambda b, i: (0,) * nd)


def _mixer(x, mod, w, state, past, layer, done=None):
    bsz, t, _ = x.shape
    p_len = 0 if past is None else past[0].shape[3]
    n_done = 0 if done is None else done[0].shape[0]
    tm, ta, tk, tkd, n_tiles = _mixer_tiles(t, p_len)
    tri_d = _tri_ones(tkd)
    tri_k = _tri_ones(tk)
    cs = _cumsum_mats(min(tm, KEY_BLOCK))
    per_b = lambda b, i: (b, 0, 0)
    tile = lambda b, i: (b, i, 0)
    tile_t = lambda b, i: (0, b, 0, i)

    consts = [w["ln1_g"], w["w_in"], w["wkt"], w["wvt"], w["wa2"], w["gla_ba"], w["rg_conv_w"],
              w["rg_conv_b"], w["rg_wa"], w["rg_ba"], w["rg_wx"], w["rg_bx"], w["rg_lambda"],
              w["sc_conv_w"], w["out_norm_g"], w["w_out"], w["hnm"], cs, tri_d, tri_k]
    in_specs = [pl.BlockSpec((1, tm, D_MODEL), tile), pl.BlockSpec((1, 1, N_MOD), per_b)]
    in_specs += [_const_spec(c.shape) for c in consts]
    in_specs += [pl.BlockSpec((1,) + s.shape[1:], per_b) for s in state]
    args = [x, mod] + consts + list(state)
    if past is not None:
        in_specs += [pl.BlockSpec((1, 1, GROUP_W, p_len), lambda b, i: (layer, b, 0, 0))] * 2
        args += list(past)
    if n_done:
        in_specs += [pl.BlockSpec((n_done, 1, GROUP_W, tm), tile_t)] * 2
        args += list(done)

    out_shape = [
        jax.ShapeDtypeStruct((bsz, t, D_MODEL), F32),
        jax.ShapeDtypeStruct((n_done + 1, bsz, GROUP_W, t), F32),
        jax.ShapeDtypeStruct((n_done + 1, bsz, GROUP_W, t), F32),
    ] + [jax.ShapeDtypeStruct(s.shape, F32) for s in state]
    out_specs = [pl.BlockSpec((1, tm, D_MODEL), tile),
                 pl.BlockSpec((n_done + 1, 1, GROUP_W, tm), tile_t),
                 pl.BlockSpec((n_done + 1, 1, GROUP_W, tm), tile_t)]
    out_specs += [pl.BlockSpec((1,) + s.shape[1:], per_b) for s in state]

    scratch = [
        pltpu.VMEM((max(tm, tkd), D_MODEL), BF16),
        pltpu.VMEM((GROUP_W, N_HEADS * GLA_DK), F32),
        pltpu.VMEM((1, GROUP_W), F32),
        pltpu.VMEM((SUBLANE, GROUP_W), F32),
        pltpu.VMEM((SUBLANE, GROUP_W), F32),
        pltpu.VMEM((tm // ta, N_HEADS * ta, GROUP_W), BF16),
        pltpu.VMEM((tm, GROUP_W), F32),
        pltpu.VMEM((tm // ta, N_HEADS * ta, LANE), F32),
    ]
    if n_tiles > 1:
        scratch += [pltpu.VMEM((t // tk, GROUP_W, tk), BF16)] * 2

    kern = functools.partial(_mixer_kernel, tm=tm, ta=ta, tk=tk, tkd=tkd, n_tiles=n_tiles,
                             p_len=p_len, n_done=n_done)
    return pl.pallas_call(
        kern,
        grid=(bsz, n_tiles),
        in_specs=in_specs,
        out_specs=out_specs,
        out_shape=out_shape,
        scratch_shapes=scratch,
        compiler_params=pltpu.CompilerParams(
            dimension_semantics=("arbitrary", "arbitrary"), vmem_limit_bytes=VMEM_LIMIT_BYTES),
        name="mixer_t%d_p%d_l%d" % (t, p_len, n_done),
    )(*args)


def _mlp_kernel(x_ref, mod_ref, g_ref, w1_ref, w2_ref, fg_ref, o_ref, *, final):
    x = x_ref[...]
    bb, tm, _ = x.shape
    mod = mod_ref[...]
    sh2 = mod[:, :, 3 * D_MODEL:4 * D_MODEL]
    sc2 = mod[:, :, 4 * D_MODEL:5 * D_MODEL]
    g2 = mod[:, :, 5 * D_MODEL:6 * D_MODEL]
    h2 = ((_rms(x) * g_ref[...]) * (1.0 + sc2) + sh2).astype(BF16).reshape(bb * tm, D_MODEL)
    fc = D_FF // MLP_FF_CHUNKS
    acc = None
    for c in range(MLP_FF_CHUNKS):
        u = jnp.maximum(_dot(h2, w1_ref[:, c * fc:(c + 1) * fc]), 0.0)
        part = _dot((u * u).astype(BF16), w2_ref[c * fc:(c + 1) * fc, :])
        acc = part if acc is None else acc + part
    y = x + g2 * acc.reshape(bb, tm, D_MODEL)
    if final:
        y = _rms(y) * fg_ref[...]
    o_ref[...] = y


def _mlp(x, mod, ln2_g, w1, w2, final_g, final):
    bsz, t, _ = x.shape
    tm = min(t, MLP_ROWS)
    bb = 1 if tm == MLP_ROWS else min(bsz, MLP_ROWS // tm)
    assert t % tm == 0 and bsz % bb == 0
    tile = lambda b, i: (b, i, 0)
    return pl.pallas_call(
        functools.partial(_mlp_kernel, final=final),
        grid=(bsz // bb, t // tm),
        in_specs=[
            pl.BlockSpec((bb, tm, D_MODEL), tile),
            pl.BlockSpec((bb, 1, N_MOD), lambda b, i: (b, 0, 0)),
            _const_spec(ln2_g.shape),
            pl.BlockSpec(w1.shape, lambda b, i: (0, 0), pipeline_mode=pl.Buffered(1)),
            pl.BlockSpec(w2.shape, lambda b, i: (0, 0), pipeline_mode=pl.Buffered(1)),
            _const_spec(final_g.shape),
        ],
        out_specs=pl.BlockSpec((bb, tm, D_MODEL), tile),
        out_shape=jax.ShapeDtypeStruct(x.shape, F32),
        compiler_params=pltpu.CompilerParams(
            dimension_semantics=("parallel", "parallel"), vmem_limit_bytes=VMEM_LIMIT_BYTES),
        name="mlp_t%d%s" % (t, "_final" if final else ""),
    )(x, mod, ln2_g, w1, w2, final_g)


def _block_diag(blocks):
    n, a, b = blocks.shape
    eye = jnp.eye(n, dtype=blocks.dtype)
    return jnp.einsum("nij,nm->nimj", blocks, eye).reshape(n * a, n * b)


def _layer_weights(l, ln1_g, w_in, gla_wa2, gla_ba, rg_conv_w, rg_conv_b, rg_wa, rg_ba, rg_wx,
                   rg_bx, rg_lambda, sc_conv_w, out_norm_g, w_out):
    wi = w_in[l]
    w_in_p = jnp.concatenate(
        [wi[:, :LRA_END], jnp.zeros((D_MODEL, LANE - GLA_RANK), wi.dtype), wi[:, LRA_END:SBK_START],
         wi[:, SBV_END:]], axis=1)
    wa2 = jnp.zeros((LANE, N_HEADS * GLA_DK), F32).at[:GLA_RANK].set(gla_wa2[l])
    hnm = _block_diag(jnp.full((N_HEADS, HEAD_DIM, HEAD_DIM), 1.0 / HEAD_DIM, F32))
    row = lambda v: v[l].reshape(1, -1)
    return {
        "ln1_g": row(ln1_g), "w_in": w_in_p.astype(BF16),
        "wkt": wi[:, SBK_START:SBV_START].T.astype(BF16), "wvt": wi[:, SBV_START:SBV_END].T.astype(BF16),
        "wa2": wa2.astype(BF16),
        "gla_ba": row(gla_ba), "rg_conv_w": rg_conv_w[l], "rg_conv_b": row(rg_conv_b),
        "rg_wa": _block_diag(rg_wa[l]).astype(BF16), "rg_ba": row(rg_ba),
        "rg_wx": _block_diag(rg_wx[l]).astype(BF16), "rg_bx": row(rg_bx),
        "rg_lambda": row(rg_lambda), "sc_conv_w": sc_conv_w[l], "out_norm_g": row(out_norm_g),
        "w_out": w_out[l].astype(BF16), "hnm": hnm.astype(BF16),
    }


def _state_in(s_gla, h_rg, buf_rg, buf_sc):
    bsz = s_gla.shape[0]
    eye = jnp.eye(N_HEADS, dtype=F32)
    st = jnp.einsum("bhkv,hg->bhvgk", s_gla.astype(F32), eye).reshape(bsz, GROUP_W, N_HEADS * GLA_DK)
    pad8 = lambda buf: jnp.concatenate(
        [jnp.zeros((bsz, SUBLANE - buf.shape[1], GROUP_W), F32), buf.astype(F32)], axis=1)
    return st, h_rg.astype(F32).reshape(bsz, 1, GROUP_W), pad8(buf_rg), pad8(buf_sc)


def _state_out(st, h, rgbuf, scbuf):
    bsz = st.shape[0]
    st5 = st.reshape(bsz, N_HEADS, HEAD_DIM, N_HEADS, GLA_DK)
    s_gla = jnp.stack([st5[:, h_, :, h_, :] for h_ in range(N_HEADS)], axis=1)
    return (jnp.swapaxes(s_gla, 2, 3), h.reshape(bsz, GROUP_W),
            rgbuf[:, SUBLANE - (RG_CONV - 1):], scbuf[:, SUBLANE - (SC_CONV - 1):])


def _rows_from_transposed(xt):
    l, b, _, t = xt.shape
    return jnp.transpose(xt.reshape(l, b, N_HEADS, HEAD_DIM, t), (0, 1, 4, 2, 3))


def kernel(x_prompt, x_sample, c_prompt, c_sample, state_gla, state_rg_h, state_rg_conv, cache_sb_k, cache_sb_v, state_sc_conv, ln1_g, ln2_g, w_ada, b_ada, w_in, gla_wa2, gla_ba, rg_conv_w, rg_conv_b, rg_wa, rg_ba, rg_wx, rg_bx, rg_lambda, sc_conv_w, out_norm_g, w_out, mlp_w1, mlp_w2, final_g):
    depth = w_in.shape[0]
    bp, bs = x_prompt.shape[0], x_sample.shape[0]
    p_len = cache_sb_k.shape[2]

    rows = -(-(bp + bs) // 16) * 16
    c_all = jnp.concatenate(
        [c_prompt, c_sample, jnp.zeros((rows - bp - bs, D_MODEL), F32)], axis=0)
    mod_all = _modulation(c_all, w_ada, b_ada)

    zero_state = _state_in(jnp.zeros((bp, N_HEADS, GLA_DK, HEAD_DIM), F32), jnp.zeros((bp, GROUP_W), F32),
                           jnp.zeros((bp, RG_CONV - 1, GROUP_W), F32), jnp.zeros((bp, SC_CONV - 1, GROUP_W), F32))
    fg = final_g.reshape(1, D_MODEL)
    past = tuple(jnp.transpose(c, (0, 1, 3, 4, 2)).reshape(depth, bs, GROUP_W, p_len)
                 for c in (cache_sb_k, cache_sb_v))

    xp, xs = x_prompt, x_sample
    kv_p, kv_s = None, None
    states_p, states_s = [], []
    for l in range(depth):
        w = _layer_weights(l, ln1_g, w_in, gla_wa2, gla_ba, rg_conv_w, rg_conv_b, rg_wa, rg_ba,
                           rg_wx, rg_bx, rg_lambda, sc_conv_w, out_norm_g, w_out)
        w1 = mlp_w1[l].astype(BF16)
        w2 = mlp_w2[l].astype(BF16)
        g2 = ln2_g[l].reshape(1, D_MODEL)
        mod_p = mod_all[l, :bp].reshape(bp, 1, N_MOD)
        mod_s = mod_all[l, bp:bp + bs].reshape(bs, 1, N_MOD)
        final = l == depth - 1

        rp = _mixer(xp, mod_p, w, zero_state, None, l, kv_p)
        sample_state = _state_in(state_gla[l], state_rg_h[l], state_rg_conv[l], state_sc_conv[l])
        rs = _mixer(xs, mod_s, w, sample_state, past, l, kv_s)
        kv_p, kv_s = (rp[1], rp[2]), (rs[1], rs[2])

        xp = _mlp(rp[0], mod_p, g2, w1, w2, fg, final)
        xs = _mlp(rs[0], mod_s, g2, w1, w2, fg, final)
        states_p.append(_state_out(*rp[3:7]))
        states_s.append(_state_out(*rs[3:7]))

    def stk(lst, j):
        return jnp.stack([st[j] for st in lst], axis=0)

    res = [xp, xs]
    for j in range(3):
        res += [stk(states_p, j), stk(states_s, j)]
    res += [_rows_from_transposed(kv_p[0]), _rows_from_transposed(kv_s[0]),
            _rows_from_transposed(kv_p[1]), _rows_from_transposed(kv_s[1]),
            stk(states_p, 3), stk(states_s, 3)]
    return tuple(res)
```
